```python
import jax, jax.numpy as jnp
from jax import lax
import numpy as np

D_MODEL = 1024
BATCH = 4
SEQ = 4096
DEPTH = 4

CHUNK = 64
N_MIXERS = 3
D_FF = 4 * D_MODEL
CONV_WIDTH = 4
RMS_EPS = 1e-6

LRU_WIDTH = D_MODEL
LRU_BLOCK_DIM = 256
LRU_BLOCKS = LRU_WIDTH // LRU_BLOCK_DIM
LRU_C = 8.0

RET_DK = 256
RET_HEADS = D_MODEL // RET_DK
RET_DV = 2 * RET_DK
RET_QK_WIDTH = RET_HEADS * RET_DK
RET_V_WIDTH = RET_HEADS * RET_DV
RET_IN_WIDTH = 2 * RET_QK_WIDTH + 2 * RET_V_WIDTH
ROPE_BASE = 10000.0

GDN_DK = 128
GDN_DV = 128
GDN_QK_HEADS = D_MODEL // GDN_DK
GDN_V_HEADS = 2 * GDN_QK_HEADS
GDN_QK_WIDTH = GDN_QK_HEADS * GDN_DK
GDN_V_WIDTH = GDN_V_HEADS * GDN_DV
GDN_CONV_CH = 2 * GDN_QK_WIDTH + GDN_V_WIDTH
GDN_IN_WIDTH = GDN_CONV_CH + GDN_V_WIDTH + 2 * GDN_V_HEADS

kernel_name = "hybrid_rglru_retention_gdn_trunk"


def _n_layers_of(kind):
    return len(range(kind, DEPTH, N_MIXERS))


def rms_norm(x, g=None, eps=RMS_EPS):
    xf = x.astype(jnp.float32)
    y = xf * lax.rsqrt(jnp.mean(xf * xf, axis=-1, keepdims=True) + eps)
    if g is not None:
        y = y * g.astype(jnp.float32)
    return y.astype(x.dtype)


def l2_norm(x, eps=1e-6):
    xf = x.astype(jnp.float32)
    return xf * lax.rsqrt(jnp.sum(xf * xf, axis=-1, keepdims=True) + eps)


def causal_depthwise_conv(x, w):
    c = x.shape[-1]
    return lax.conv_general_dilated(
        x, w[:, None, :].astype(x.dtype), window_strides=(1,),
        padding=[(CONV_WIDTH - 1, 0)], dimension_numbers=('NWC', 'WIO', 'NWC'),
        feature_group_count=c)


def sq_relu_mlp(h, w_up, w_down):
    return jnp.square(jax.nn.relu(h @ w_up)) @ w_down


def _linear_combine(c1, c2):
    a1, b1 = c1
    a2, b2 = c2
    return a1 * a2, a2 * b1 + b2


def rglru_mixer(h, w_in, conv_w, conv_b, wa, ba, wx, bx, lam, w_out):
    b_, s_, _ = h.shape
    gate, xr = jnp.split(h @ w_in, 2, axis=-1)
    xr = causal_depthwise_conv(xr, conv_w) + conv_b
    xb = xr.reshape(b_, s_, LRU_BLOCKS, LRU_BLOCK_DIM)
    r = jax.nn.sigmoid(jnp.einsum('bsgi,gij->bsgj', xb, wa) + ba).reshape(b_, s_, LRU_WIDTH)
    i = jax.nn.sigmoid(jnp.einsum('bsgi,gij->bsgj', xb, wx) + bx).reshape(b_, s_, LRU_WIDTH)
    log_a = -LRU_C * r.astype(jnp.float32) * jax.nn.softplus(-lam.astype(jnp.float32))
    a = jnp.exp(log_a)
    u = jnp.sqrt(-jnp.expm1(2.0 * log_a)) * (i * xr).astype(jnp.float32)
    _, hs = lax.associative_scan(_linear_combine, (a, u), axis=1)
    y = jax.nn.gelu(gate) * hs.astype(gate.dtype)
    return y @ w_out


def rope_tables(s_):
    pos = jnp.arange(s_, dtype=jnp.float32)
    inv_freq = ROPE_BASE ** (-jnp.arange(0, RET_DK, 2, dtype=jnp.float32) / RET_DK)
    ang = pos[:, None] * inv_freq[None, :]
    return jnp.cos(ang), jnp.sin(ang)


def apply_rope(x, cos, sin):
    x1, x2 = jnp.split(x, 2, axis=-1)
    c = cos[None, :, None, :]
    s = sin[None, :, None, :]
    return jnp.concatenate([x1 * c - x2 * s, x2 * c + x1 * s], axis=-1)


def retention_chunkwise(q, k, v):
    b_, s_, nh, dk = q.shape
    dv = v.shape[-1]
    nc = s_ // CHUNK
    log_gamma = jnp.log1p(-jnp.exp2(-5.0 - jnp.arange(nh, dtype=jnp.float32)))
    pos = jnp.arange(CHUNK, dtype=jnp.float32)
    diff = pos[:, None] - pos[None, :]
    dmask = jnp.where(diff >= 0, jnp.exp(log_gamma[:, None, None] * jnp.maximum(diff, 0.0)), 0.0)
    inter_dec = jnp.exp(log_gamma[:, None] * (pos + 1.0)).T[None, :, :, None]
    k_dec = jnp.exp(log_gamma[:, None] * (CHUNK - 1.0 - pos)).T[None, :, :, None]
    chunk_dec = jnp.exp(log_gamma * CHUNK)[None, :, None, None]

    def to_chunks(t):
        return t.reshape(b_, nc, CHUNK, nh, t.shape[-1]).transpose(1, 0, 2, 3, 4)

    def body(state, xs):
        qc, kc, vc = xs
        scores = jnp.einsum('bihd,bjhd->bhij', qc, kc) * dmask
        o = (jnp.einsum('bhij,bjhe->bihe', scores, vc)
             + jnp.einsum('bihd,bhde->bihe', qc, state) * inter_dec)
        state = state * chunk_dec + jnp.einsum('bjhd,bjhe->bhde', kc * k_dec, vc)
        return state, o

    s0 = jnp.zeros((b_, nh, dk, dv), jnp.float32)
    _, o = lax.scan(body, s0, (to_chunks(q), to_chunks(k), to_chunks(v)))
    return o.transpose(1, 0, 2, 3, 4).reshape(b_, s_, nh, dv)


def retention_mixer(h, w_in, w_out, cos, sin):
    b_, s_, _ = h.shape
    q, k, v, g = jnp.split(h @ w_in, [RET_QK_WIDTH, 2 * RET_QK_WIDTH,
                                      2 * RET_QK_WIDTH + RET_V_WIDTH], axis=-1)
    q = apply_rope(q.reshape(b_, s_, RET_HEADS, RET_DK).astype(jnp.float32), cos, sin)
    k = apply_rope(k.reshape(b_, s_, RET_HEADS, RET_DK).astype(jnp.float32), cos, sin) * (RET_DK ** -0.5)
    v = v.reshape(b_, s_, RET_HEADS, RET_DV).astype(jnp.float32)
    o = rms_norm(retention_chunkwise(q, k, v))
    o = o.reshape(b_, s_, RET_V_WIDTH).astype(h.dtype)
    return (jax.nn.silu(g) * o) @ w_out


def _unit_lower_solve(a, b):
    return lax.linalg.triangular_solve(a, b, left_side=True, lower=True, unit_diagonal=True)


def gated_delta_chunkwise(q, k, v, g, beta):
    b_, s_, nh, dk = q.shape
    dv = v.shape[-1]
    nc = s_ // CHUNK

    def to_chunks(t):
        return t.reshape(b_, nc, CHUNK, nh, t.shape[-1]).transpose(0, 3, 1, 2, 4)

    qc, kc, vc = to_chunks(q), to_chunks(k), to_chunks(v)
    gc = g.reshape(b_, nc, CHUNK, nh).transpose(0, 3, 1, 2)
    bc = beta.reshape(b_, nc, CHUNK, nh).transpose(0, 3, 1, 2)
    G = jnp.cumsum(gc, axis=-1)
    causal = jnp.tril(jnp.ones((CHUNK, CHUNK), dtype=bool))
    strict = jnp.tril(jnp.ones((CHUNK, CHUNK), dtype=bool), k=-1)
    decay = jnp.exp(jnp.where(causal, G[..., :, None] - G[..., None, :], -jnp.inf))
    k_beta = kc * bc[..., None]
    v_beta = vc * bc[..., None]
    a_mat = jnp.where(strict, jnp.einsum('bhncd,bhnjd->bhncj', k_beta, kc) * decay, 0.0)
    u = _unit_lower_solve(a_mat, v_beta)
    w = _unit_lower_solve(a_mat, k_beta * jnp.exp(G)[..., None])
    attn = jnp.einsum('bhncd,bhnjd->bhncj', qc, kc) * decay
    q_dec = qc * jnp.exp(G)[..., None]
    k_dec = kc * jnp.exp(G[..., -1:] - G)[..., None]
    g_tot = jnp.exp(G[..., -1])

    def body(state, xs):
        q_i, w_i, u_i, attn_i, k_i, gt_i = xs
        v_new = u_i - jnp.einsum('bhcd,bhde->bhce', w_i, state)
        o = (jnp.einsum('bhcd,bhde->bhce', q_i, state)
             + jnp.einsum('bhcj,bhje->bhce', attn_i, v_new))
        state = state * gt_i[..., None, None] + jnp.einsum('bhcd,bhce->bhde', k_i, v_new)
        return state, o

    xs = tuple(jnp.moveaxis(t, 2, 0) for t in (q_dec, w, u, attn, k_dec, g_tot))
    s0 = jnp.zeros((b_, nh, dk, dv), jnp.float32)
    _, o = lax.scan(body, s0, xs)
    return o.transpose(1, 0, 3, 2, 4).reshape(b_, s_, nh, dv)


def gdn_mixer(h, w_in, conv_w, a_log, dt_bias, norm_w, w_out):
    b_, s_, _ = h.shape
    qkv, z, b_logit, a_in = jnp.split(h @ w_in, [GDN_CONV_CH, GDN_CONV_CH + GDN_V_WIDTH,
                                                 GDN_CONV_CH + GDN_V_WIDTH + GDN_V_HEADS], axis=-1)
    qkv = jax.nn.silu(causal_depthwise_conv(qkv, conv_w))
    q, k, v = jnp.split(qkv, [GDN_QK_WIDTH, 2 * GDN_QK_WIDTH], axis=-1)
    rep = GDN_V_HEADS // GDN_QK_HEADS
    q = jnp.repeat(l2_norm(q.reshape(b_, s_, GDN_QK_HEADS, GDN_DK)), rep, axis=2) * (GDN_DK ** -0.5)
    k = jnp.repeat(l2_norm(k.reshape(b_, s_, GDN_QK_HEADS, GDN_DK)), rep, axis=2)
    v = v.reshape(b_, s_, GDN_V_HEADS, GDN_DV).astype(jnp.float32)
    beta = jax.nn.sigmoid(b_logit.astype(jnp.float32))
    g = -jnp.exp(a_log.astype(jnp.float32)) * jax.nn.softplus(a_in.astype(jnp.float32) + dt_bias.astype(jnp.float32))
    o = gated_delta_chunkwise(q, k, v, g, beta)
    o = rms_norm(o, norm_w) * jax.nn.silu(z.reshape(b_, s_, GDN_V_HEADS, GDN_DV).astype(jnp.float32))
    return o.reshape(b_, s_, GDN_V_WIDTH).astype(h.dtype) @ w_out


def setup_inputs(seed: int = 0) -> dict:
    key = jax.random.key(seed)
    ks = iter(jax.random.split(key, 32))
    f32 = jnp.float32
    n_a, n_b, n_c = _n_layers_of(0), _n_layers_of(1), _n_layers_of(2)

    def nrm(shape, scale):
        return jax.random.normal(next(ks), shape, f32) * scale

    def gain(shape):
        return 1.0 + nrm(shape, 0.02)

    x = nrm((BATCH, SEQ, D_MODEL), 1.0)
    mix_norm = gain((DEPTH, D_MODEL))
    mlp_norm = gain((DEPTH, D_MODEL))
    mlp_w_up = nrm((DEPTH, D_MODEL, D_FF), D_MODEL ** -0.5)
    mlp_w_down = nrm((DEPTH, D_FF, D_MODEL), 0.5 * D_FF ** -0.5)

    lru_w_in = nrm((n_a, D_MODEL, 2 * LRU_WIDTH), D_MODEL ** -0.5)
    lru_conv_w = nrm((n_a, CONV_WIDTH, LRU_WIDTH), CONV_WIDTH ** -0.5)
    lru_conv_b = nrm((n_a, LRU_WIDTH), 0.01)
    lru_wa = nrm((n_a, LRU_BLOCKS, LRU_BLOCK_DIM, LRU_BLOCK_DIM), LRU_BLOCK_DIM ** -0.5)
    lru_ba = nrm((n_a, LRU_BLOCKS, LRU_BLOCK_DIM), 0.01)
    lru_wx = nrm((n_a, LRU_BLOCKS, LRU_BLOCK_DIM, LRU_BLOCK_DIM), LRU_BLOCK_DIM ** -0.5)
    lru_bx = nrm((n_a, LRU_BLOCKS, LRU_BLOCK_DIM), 0.01)
    s_rad = jnp.sqrt(jax.random.uniform(next(ks), (n_a, LRU_WIDTH), f32, 0.9 ** 2, 0.999 ** 2))
    lru_lambda = jnp.log(s_rad) - jnp.log1p(-s_rad)
    lru_w_out = nrm((n_a, LRU_WIDTH, D_MODEL), LRU_WIDTH ** -0.5)

    ret_w_in = nrm((n_b, D_MODEL, RET_IN_WIDTH), D_MODEL ** -0.5)
    ret_w_out = nrm((n_b, RET_V_WIDTH, D_MODEL), RET_V_WIDTH ** -0.5)

    gdn_w_in = nrm((n_c, D_MODEL, GDN_IN_WIDTH), D_MODEL ** -0.5)
    gdn_conv_w = nrm((n_c, CONV_WIDTH, GDN_CONV_CH), CONV_WIDTH ** -0.5)
    gdn_a_log = jnp.log(jax.random.uniform(next(ks), (n_c, GDN_V_HEADS), f32, 1.0, 16.0))
    dt = jnp.exp(jax.random.uniform(next(ks), (n_c, GDN_V_HEADS), f32, np.log(0.001), np.log(0.1)))
    gdn_dt_bias = dt + jnp.log(-jnp.expm1(-dt))
    gdn_norm = gain((n_c, GDN_DV))
    gdn_w_out = nrm((n_c, GDN_V_WIDTH, D_MODEL), GDN_V_WIDTH ** -0.5)

    final_norm = gain((D_MODEL,))
    return {
        'x': x, 'mix_norm': mix_norm, 'mlp_norm': mlp_norm,
        'mlp_w_up': mlp_w_up, 'mlp_w_down': mlp_w_down,
        'lru_w_in': lru_w_in, 'lru_conv_w': lru_conv_w, 'lru_conv_b': lru_conv_b,
        'lru_wa': lru_wa, 'lru_ba': lru_ba, 'lru_wx': lru_wx, 'lru_bx': lru_bx,
        'lru_lambda': lru_lambda, 'lru_w_out': lru_w_out,
        'ret_w_in': ret_w_in, 'ret_w_out': ret_w_out,
        'gdn_w_in': gdn_w_in, 'gdn_conv_w': gdn_conv_w, 'gdn_a_log': gdn_a_log,
        'gdn_dt_bias': gdn_dt_bias, 'gdn_norm': gdn_norm, 'gdn_w_out': gdn_w_out,
        'final_norm': final_norm,
    }


def reference(x, mix_norm, mlp_norm, mlp_w_up, mlp_w_down,
              lru_w_in, lru_conv_w, lru_conv_b, lru_wa, lru_ba, lru_wx, lru_bx,
              lru_lambda, lru_w_out,
              ret_w_in, ret_w_out,
              gdn_w_in, gdn_conv_w, gdn_a_log, gdn_dt_bias, gdn_norm, gdn_w_out,
              final_norm):
    _, s_, _ = x.shape
    cos, sin = rope_tables(s_)
    h = x
    for i in range(DEPTH):
        kind, j = i % N_MIXERS, i // N_MIXERS
        u = rms_norm(h, mix_norm[i])
        if kind == 0:
            m = rglru_mixer(u, lru_w_in[j], lru_conv_w[j], lru_conv_b[j], lru_wa[j], lru_ba[j],
                            lru_wx[j], lru_bx[j], lru_lambda[j], lru_w_out[j])
        elif kind == 1:
            m = retention_mixer(u, ret_w_in[j], ret_w_out[j], cos, sin)
        else:
            m = gdn_mixer(u, gdn_w_in[j], gdn_conv_w[j], gdn_a_log[j], gdn_dt_bias[j],
                          gdn_norm[j], gdn_w_out[j])
        h = h + m
        h = h + sq_relu_mlp(rms_norm(h, mlp_norm[i]), mlp_w_up[i], mlp_w_down[i])
    return rms_norm(h, final_norm)
```

```python
import functools
import math

import jax
import jax.numpy as jnp
from jax import lax
from jax.experimental import pallas as pl
from jax.experimental.pallas import tpu as pltpu

F32 = jnp.float32
BF16 = jnp.bfloat16

RMS_EPS = 1e-6
L2_EPS = 1e-6
CONV_WIDTH = 4
LRU_C = 8.0
LRU_BLOCK = 256
RET_HEADS = 4
RET_DK = 256
RET_DV = 512
ROPE_BASE = 10000.0
GDN_QK_HEADS = 8
GDN_V_HEADS = 16
GDN_D = 128
LANES = 128
SUBLANES = 8
VMEM_LIMIT = 56 * 1024 * 1024

NT_DIMS = (((1,), (1,)), ((), ()))
TN_DIMS = (((0,), (0,)), ((), ()))


def _dot(a, b):
    return jnp.dot(a, b, preferred_element_type=F32)


def _dot_nt(a, b):
    return lax.dot_general(a, b, NT_DIMS, preferred_element_type=F32)


def _dot_tn(a, b):
    return lax.dot_general(a, b, TN_DIMS, preferred_element_type=F32)


def _rms(x):
    return x * lax.rsqrt(jnp.mean(x * x, axis=-1, keepdims=True) + RMS_EPS)


def _sigmoid(x):
    return 1.0 / (1.0 + jnp.exp(-x))


def _softplus(x):
    return jnp.maximum(x, 0.0) + jnp.log1p(jnp.exp(-jnp.abs(x)))


def _params(n_axes):
    return pltpu.CompilerParams(dimension_semantics=("arbitrary",) * n_axes,
                                vmem_limit_bytes=VMEM_LIMIT)


def _norm_matmul_kernel(h_ref, g_ref, w_ref, o_ref):
    u = (_rms(h_ref[...]) * g_ref[...]).astype(BF16)
    o_ref[...] = _dot(u, w_ref[...])


def _norm_matmul(h2, g, w, tm):
    t, d = h2.shape
    n = w.shape[1]
    return pl.pallas_call(
        _norm_matmul_kernel,
        grid=(t // tm,),
        in_specs=[pl.BlockSpec((tm, d), lambda i: (i, 0)),
                  pl.BlockSpec((1, d), lambda i: (0, 0)),
                  pl.BlockSpec((d, n), lambda i: (0, 0))],
        out_specs=pl.BlockSpec((tm, n), lambda i: (i, 0)),
        out_shape=jax.ShapeDtypeStruct((t, n), F32),
        compiler_params=_params(1),
        name="norm_matmul",
    )(h2, g.reshape(1, d), w)


def _mlp_kernel(h_ref, g_ref, wu_ref, wd_ref, fg_ref, o_ref, u_scr, acc_scr, *, final):
    j = pl.program_id(1)

    @pl.when(j == 0)
    def _():
        u_scr[...] = (_rms(h_ref[...]) * g_ref[...]).astype(BF16)
        acc_scr[...] = jnp.zeros_like(acc_scr)

    hid = _dot(u_scr[...], wu_ref[...])
    hid = jnp.square(jnp.maximum(hid, 0.0)).astype(BF16)
    acc_scr[...] += _dot(hid, wd_ref[...])

    @pl.when(j == pl.num_programs(1) - 1)
    def _():
        out = h_ref[...] + acc_scr[...]
        if final:
            out = _rms(out) * fg_ref[...]
        o_ref[...] = out


def _mlp(h2, g, wu, wd, fg, final, tm, tf):
    t, d = h2.shape
    ff = wu.shape[1]
    return pl.pallas_call(
        functools.partial(_mlp_kernel, final=final),
        grid=(t // tm, ff // tf),
        in_specs=[pl.BlockSpec((tm, d), lambda i, j: (i, 0)),
                  pl.BlockSpec((1, d), lambda i, j: (0, 0)),
                  pl.BlockSpec((d, tf), lambda i, j: (0, j)),
                  pl.BlockSpec((tf, d), lambda i, j: (j, 0)),
                  pl.BlockSpec((1, d), lambda i, j: (0, 0))],
        out_specs=pl.BlockSpec((tm, d), lambda i, j: (i, 0)),
        out_shape=jax.ShapeDtypeStruct((t, d), F32),
        scratch_shapes=[pltpu.VMEM((tm, d), BF16), pltpu.VMEM((tm, d), F32)],
        compiler_params=_params(2),
        name="mlp",
    )(h2, g.reshape(1, d), wu, wd, fg.reshape(1, d))


def _causal_conv(x, cw, xbuf, tt):
    xbuf[SUBLANES:SUBLANES + tt, :] = x
    out = cw[3:4, :] * x
    for k in range(1, CONV_WIDTH):
        out = out + cw[3 - k:4 - k, :] * xbuf[SUBLANES - k:SUBLANES - k + tt, :]
    xbuf[0:SUBLANES, :] = xbuf[tt:tt + SUBLANES, :]
    return out


def _cumsum_rows(x):
    n = x.shape[0]
    row = lax.broadcasted_iota(jnp.int32, x.shape, 0)
    d = 1
    while d < n:
        x = x + jnp.where(row >= d, pltpu.roll(x, d, 0), 0.0)
        d *= 2
    return x


def _lru_kernel(z_ref, h_ref, cw_ref, cb_ref, wa_ref, ba_ref, wx_ref, bx_ref, lam_ref,
                wout_ref, o_ref, xbuf, carry, hs_scr, *, tt, width):
    @pl.when(pl.program_id(1) == 0)
    def _():
        xbuf[0:SUBLANES, :] = jnp.zeros((SUBLANES, width), F32)
        carry[...] = jnp.zeros_like(carry)

    z = z_ref[0]
    gate = z[:, :width]
    xr = _causal_conv(z[:, width:], cw_ref[...], xbuf, tt) + cb_ref[...]
    xb = xr.astype(BF16)
    nblk = width // LRU_BLOCK
    ra = jnp.concatenate(
        [_dot(xb[:, g * LRU_BLOCK:(g + 1) * LRU_BLOCK], wa_ref[g]) for g in range(nblk)], axis=1)
    ia = jnp.concatenate(
        [_dot(xb[:, g * LRU_BLOCK:(g + 1) * LRU_BLOCK], wx_ref[g]) for g in range(nblk)], axis=1)
    r = _sigmoid(ra + ba_ref[...])
    i = _sigmoid(ia + bx_ref[...])
    log_a = -LRU_C * r * _softplus(-lam_ref[...])
    a = jnp.exp(log_a)
    b = jnp.sqrt(1.0 - a * a) * (i * xr)

    row = lax.broadcasted_iota(jnp.int32, (tt, width), 0) & (SUBLANES - 1)
    d = 1
    while d < SUBLANES:
        keep = row >= d
        b = jnp.where(keep, a * pltpu.roll(b, d, 0) + b, b)
        a = jnp.where(keep, a * pltpu.roll(a, d, 0), a)
        d *= 2
    c = carry[...]
    for g in range(tt // SUBLANES):
        sl = slice(g * SUBLANES, (g + 1) * SUBLANES)
        hg = a[sl, :] * c + b[sl, :]
        hs_scr[sl, :] = hg
        c = jnp.broadcast_to(hg[SUBLANES - 1:SUBLANES, :], (SUBLANES, width))
    carry[...] = c

    y = (jax.nn.gelu(gate) * hs_scr[...]).astype(BF16)
    o_ref[0] = h_ref[0] + _dot(y, wout_ref[...])


def _lru_layer(h, g, w_in, conv_w, conv_b, wa, ba, wx, bx, lam, w_out, tt=256):
    b_, s_, d = h.shape
    width = w_out.shape[0]
    z = _norm_matmul(h.reshape(b_ * s_, d), g, w_in.astype(BF16), 256).reshape(b_, s_, 2 * width)
    row = lambda v: v.reshape(1, width)
    const2 = lambda bi, ti: (0, 0)
    const3 = lambda bi, ti: (0, 0, 0)
    return pl.pallas_call(
        functools.partial(_lru_kernel, tt=tt, width=width),
        grid=(b_, s_ // tt),
        in_specs=[pl.BlockSpec((1, tt, 2 * width), lambda bi, ti: (bi, ti, 0)),
                  pl.BlockSpec((1, tt, d), lambda bi, ti: (bi, ti, 0)),
                  pl.BlockSpec((CONV_WIDTH, width), const2),
                  pl.BlockSpec((1, width), const2),
                  pl.BlockSpec(wa.shape, const3),
                  pl.BlockSpec((1, width), const2),
                  pl.BlockSpec(wx.shape, const3),
                  pl.BlockSpec((1, width), const2),
                  pl.BlockSpec((1, width), const2),
                  pl.BlockSpec((width, d), const2)],
        out_specs=pl.BlockSpec((1, tt, d), lambda bi, ti: (bi, ti, 0)),
        out_shape=jax.ShapeDtypeStruct((b_, s_, d), F32),
        scratch_shapes=[pltpu.VMEM((tt + SUBLANES, width), F32),
                        pltpu.VMEM((SUBLANES, width), F32),
                        pltpu.VMEM((tt, width), F32)],
        compiler_params=_params(2),
        name="lru_mixer",
    )(z, h, conv_w, row(conv_b), wa.astype(BF16), row(ba), wx.astype(BF16), row(bx), row(lam),
      w_out.astype(BF16))


def _rope(x, cos, sin):
    half = x.shape[1] // 2
    x1, x2 = x[:, :half], x[:, half:]
    return jnp.concatenate([x1 * cos - x2 * sin, x2 * cos + x1 * sin], axis=1)


def _ret_kernel(z_ref, h_ref, cos_ref, sin_ref, wout_ref, o_ref, state, y_scr, *, c):
    @pl.when(pl.program_id(1) == 0)
    def _():
        state[...] = jnp.zeros_like(state)

    cos = cos_ref[...]
    sin = sin_ref[...]
    ri = lax.broadcasted_iota(jnp.int32, (c, c), 0)
    ci = lax.broadcasted_iota(jnp.int32, (c, c), 1)
    diff = (ri - ci).astype(F32)
    pos = lax.broadcasted_iota(jnp.int32, (c, 1), 0).astype(F32)
    qk_w = RET_HEADS * RET_DK
    v_w = RET_HEADS * RET_DV
    for hd in range(RET_HEADS):
        log_gamma = math.log1p(-(2.0 ** (-5.0 - hd)))
        q = _rope(z_ref[0, :, hd * RET_DK:(hd + 1) * RET_DK], cos, sin)
        k = _rope(z_ref[0, :, qk_w + hd * RET_DK:qk_w + (hd + 1) * RET_DK], cos, sin) * (RET_DK ** -0.5)
        v = z_ref[0, :, 2 * qk_w + hd * RET_DV:2 * qk_w + (hd + 1) * RET_DV].astype(BF16)
        gt = z_ref[0, :, 2 * qk_w + v_w + hd * RET_DV:2 * qk_w + v_w + (hd + 1) * RET_DV]
        dmask = jnp.where(diff >= 0.0, jnp.exp(log_gamma * jnp.maximum(diff, 0.0)), 0.0)
        inter_dec = jnp.exp(log_gamma * (pos + 1.0))
        k_dec = jnp.exp(log_gamma * (c - 1.0 - pos))
        chunk_dec = math.exp(log_gamma * c)
        qb = q.astype(BF16)
        scores = _dot_nt(qb, k.astype(BF16)) * dmask
        s_old = state[hd]
        o = _dot(scores.astype(BF16), v) + _dot(qb, s_old.astype(BF16)) * inter_dec
        state[hd] = s_old * chunk_dec + _dot_tn((k * k_dec).astype(BF16), v)
        y = gt * _sigmoid(gt) * _rms(o)
        y_scr[:, hd * RET_DV:(hd + 1) * RET_DV] = y.astype(BF16)
    o_ref[0] = h_ref[0] + _dot(y_scr[...], wout_ref[...])


def _ret_layer(h, g, w_in, w_out, cos, sin, c=256):
    b_, s_, d = h.shape
    n_in = w_in.shape[1]
    v_w = w_out.shape[0]
    z = _norm_matmul(h.reshape(b_ * s_, d), g, w_in.astype(BF16), 256).reshape(b_, s_, n_in)
    return pl.pallas_call(
        functools.partial(_ret_kernel, c=c),
        grid=(b_, s_ // c),
        in_specs=[pl.BlockSpec((1, c, n_in), lambda bi, ti: (bi, ti, 0)),
                  pl.BlockSpec((1, c, d), lambda bi, ti: (bi, ti, 0)),
                  pl.BlockSpec((c, RET_DK // 2), lambda bi, ti: (ti, 0)),
                  pl.BlockSpec((c, RET_DK // 2), lambda bi, ti: (ti, 0)),
                  pl.BlockSpec((v_w, d), lambda bi, ti: (0, 0))],
        out_specs=pl.BlockSpec((1, c, d), lambda bi, ti: (bi, ti, 0)),
        out_shape=jax.ShapeDtypeStruct((b_, s_, d), F32),
        scratch_shapes=[pltpu.VMEM((RET_HEADS, RET_DK, RET_DV), F32),
                        pltpu.VMEM((c, v_w), BF16)],
        compiler_params=_params(2),
        name="retention_mixer",
    )(z, h, cos, sin, w_out.astype(BF16))


def _unit_lower_inverse(a, ri, ci):
    c = a.shape[0]
    same = lambda sh: (ri >> sh) == (ci >> sh)
    x = jnp.where(same(4), -a, 0.0)
    t = jnp.where(ri == ci, 1.0, 0.0) + x
    xb = x.astype(BF16)
    p = _dot(xb, xb)
    for _ in range(2):
        tp = _dot(jnp.concatenate([t, p], axis=0).astype(BF16), p.astype(BF16))
        t = t + tp[:c]
        p = tp[c:]
    t = t + _dot(t.astype(BF16), p.astype(BF16))
    sh = 4
    while (1 << sh) < c:
        low = jnp.where(same(sh + 1) & jnp.logical_not(same(sh)), a, 0.0)
        tb = t.astype(BF16)
        t = t - _dot(tb, _dot(low.astype(BF16), tb).astype(BF16))
        sh += 1
    return t


def _gdn_kernel(z_ref, zs_ref, h_ref, cw_ref, alog_ref, dtb_ref, nw_ref, wout_ref, o_ref,
                xbuf, state, y_scr, *, c):
    qk_w = GDN_QK_HEADS * GDN_D
    conv_ch = 2 * qk_w + GDN_V_HEADS * GDN_D

    @pl.when(pl.program_id(1) == 0)
    def _():
        xbuf[0:SUBLANES, :] = jnp.zeros((SUBLANES, conv_ch), F32)
        state[...] = jnp.zeros_like(state)

    xc = _causal_conv(z_ref[0, :, :conv_ch], cw_ref[...], xbuf, c)
    qkv = xc * _sigmoid(xc)

    zs = zs_ref[0]
    beta_all = _sigmoid(zs)
    g_all = -jnp.exp(alog_ref[...]) * _softplus(zs + dtb_ref[...])
    gcum = _cumsum_rows(g_all)
    gcum_t = gcum.T
    g_last = gcum[c - 1:c, :]
    exp_g = jnp.exp(gcum)
    exp_rest = jnp.exp(g_last - gcum)
    exp_tot = jnp.exp(g_last)

    ri = lax.broadcasted_iota(jnp.int32, (c, c), 0)
    ci = lax.broadcasted_iota(jnp.int32, (c, c), 1)
    causal = ri >= ci
    strict = ri > ci
    nw = nw_ref[...]
    rep = GDN_V_HEADS // GDN_QK_HEADS
    for j in range(GDN_QK_HEADS):
        qj = qkv[:, j * GDN_D:(j + 1) * GDN_D]
        kj = qkv[:, qk_w + j * GDN_D:qk_w + (j + 1) * GDN_D]
        qn = qj * lax.rsqrt(jnp.sum(qj * qj, axis=-1, keepdims=True) + L2_EPS) * (GDN_D ** -0.5)
        kn = kj * lax.rsqrt(jnp.sum(kj * kj, axis=-1, keepdims=True) + L2_EPS)
        prod = _dot_nt(jnp.concatenate([kn, qn], axis=0).astype(BF16), kn.astype(BF16))
        kk = prod[:c]
        qk = prod[c:]
        for r in range(rep):
            vh = rep * j + r
            lane = GDN_V_HEADS + vh
            v = qkv[:, 2 * qk_w + vh * GDN_D:2 * qk_w + (vh + 1) * GDN_D]
            beta = beta_all[:, vh:vh + 1]
            g_col = gcum[:, lane:lane + 1]
            g_row = gcum_t[lane:lane + 1, :]
            decay = jnp.where(causal, jnp.exp(jnp.minimum(g_col - g_row, 0.0)), 0.0)
            a_mat = jnp.where(strict, beta * kk * decay, 0.0)
            t_inv = _unit_lower_inverse(a_mat, ri, ci)
            eg = exp_g[:, lane:lane + 1]
            rhs = jnp.concatenate([v * beta, kn * (beta * eg)], axis=1)
            uw = _dot(t_inv.astype(BF16), rhs.astype(BF16))
            u = uw[:, :GDN_D]
            w = uw[:, GDN_D:]
            s_old = state[vh]
            wq = _dot(jnp.concatenate([w, qn * eg], axis=0).astype(BF16), s_old.astype(BF16))
            v_new = u - wq[:c]
            v_new_b = v_new.astype(BF16)
            o = wq[c:] + _dot((qk * decay).astype(BF16), v_new_b)
            k_dec = kn * exp_rest[:, lane:lane + 1]
            state[vh] = s_old * exp_tot[:, lane:lane + 1] + _dot_tn(k_dec.astype(BF16), v_new_b)
            zg = z_ref[0, :, conv_ch + vh * GDN_D:conv_ch + (vh + 1) * GDN_D]
            y = _rms(o) * nw * (zg * _sigmoid(zg))
            y_scr[:, vh * GDN_D:(vh + 1) * GDN_D] = y.astype(BF16)
    o_ref[0] = h_ref[0] + _dot(y_scr[...], wout_ref[...])


def _gdn_layer(h, g, w_in, conv_w, a_log, dt_bias, norm_w, w_out, c=128):
    b_, s_, d = h.shape
    v_w = w_out.shape[0]
    conv_ch = conv_w.shape[1]
    main_w = conv_ch + v_w
    small = w_in.shape[1] - main_w
    w_cat = jnp.concatenate(
        [w_in, jnp.zeros((d, LANES - small), w_in.dtype)], axis=1).astype(BF16)
    n_cat = main_w + LANES
    z = _norm_matmul(h.reshape(b_ * s_, d), g, w_cat, 256).reshape(b_, s_, n_cat)
    pad_row = lambda v: jnp.zeros((1, LANES), F32).at[0, GDN_V_HEADS:2 * GDN_V_HEADS].set(v)
    const2 = lambda bi, ti: (0, 0)
    return pl.pallas_call(
        functools.partial(_gdn_kernel, c=c),
        grid=(b_, s_ // c),
        in_specs=[pl.BlockSpec((1, c, main_w), lambda bi, ti: (bi, ti, 0)),
                  pl.BlockSpec((1, c, LANES), lambda bi, ti: (bi, ti, main_w // LANES)),
                  pl.BlockSpec((1, c, d), lambda bi, ti: (bi, ti, 0)),
                  pl.BlockSpec((CONV_WIDTH, conv_ch), const2),
                  pl.BlockSpec((1, LANES), const2),
                  pl.BlockSpec((1, LANES), const2),
                  pl.BlockSpec((1, GDN_D), const2),
                  pl.BlockSpec((v_w, d), const2)],
        out_specs=pl.BlockSpec((1, c, d), lambda bi, ti: (bi, ti, 0)),
        out_shape=jax.ShapeDtypeStruct((b_, s_, d), F32),
        scratch_shapes=[pltpu.VMEM((c + SUBLANES, conv_ch), F32),
                        pltpu.VMEM((GDN_V_HEADS, GDN_D, GDN_D), F32),
                        pltpu.VMEM((c, v_w), BF16)],
        compiler_params=_params(2),
        name="gdn_mixer",
    )(z, z, h, conv_w, pad_row(a_log), pad_row(dt_bias), norm_w.reshape(1, GDN_D),
      w_out.astype(BF16))


def kernel(x, mix_norm, mlp_norm, mlp_w_up, mlp_w_down, lru_w_in, lru_conv_w, lru_conv_b, lru_wa, lru_ba, lru_wx, lru_bx, lru_lambda, lru_w_out, ret_w_in, ret_w_out, gdn_w_in, gdn_conv_w, gdn_a_log, gdn_dt_bias, gdn_norm, gdn_w_out, final_norm):
    b_, s_, d = x.shape
    depth = mix_norm.shape[0]
    pos = jnp.arange(s_, dtype=F32)
    inv_freq = ROPE_BASE ** (-jnp.arange(0, RET_DK, 2, dtype=F32) / RET_DK)
    ang = pos[:, None] * inv_freq[None, :]
    cos, sin = jnp.cos(ang), jnp.sin(ang)
    h = x
    for i in range(depth):
        kind, j = i % 3, i // 3
        if kind == 0:
            h = _lru_layer(h, mix_norm[i], lru_w_in[j], lru_conv_w[j], lru_conv_b[j], lru_wa[j],
                           lru_ba[j].reshape(-1), lru_wx[j], lru_bx[j].reshape(-1), lru_lambda[j],
                           lru_w_out[j])
        elif kind == 1:
            h = _ret_layer(h, mix_norm[i], ret_w_in[j], ret_w_out[j], cos, sin)
        else:
            h = _gdn_layer(h, mix_norm[i], gdn_w_in[j], gdn_conv_w[j], gdn_a_log[j],
                           gdn_dt_bias[j], gdn_norm[j], gdn_w_out[j])
        h = _mlp(h.reshape(b_ * s_, d), mlp_norm[i], mlp_w_up[i].astype(BF16),
                 mlp_w_down[i].astype(BF16), final_norm, i == depth - 1, 1024, 1024
                 ).reshape(b_, s_, d)
    return h
```

```python
import functools
import math

import jax
import jax.numpy as jnp
from jax import lax
from jax.experimental import pallas as pl
from jax.experimental.pallas import tpu as pltpu

F32 = jnp.float32
BF16 = jnp.bfloat16

RMS_EPS = 1e-6
L2_EPS = 1e-6
CONV_WIDTH = 4
LRU_C = 8.0
LRU_BLOCK = 256
RET_HEADS = 4
RET_DK = 256
RET_DV = 512
ROPE_BASE = 10000.0
GDN_QK_HEADS = 8
GDN_V_HEADS = 16
GDN_D = 128
LANES = 128
SUBLANES = 8
VMEM_LIMIT = 56 * 1024 * 1024

NT_DIMS = (((1,), (1,)), ((), ()))
TN_DIMS = (((0,), (0,)), ((), ()))


def _dot(a, b):
    return jnp.dot(a, b, preferred_element_type=F32)


def _dot_nt(a, b):
    return lax.dot_general(a, b, NT_DIMS, preferred_element_type=F32)


def _dot_tn(a, b):
    return lax.dot_general(a, b, TN_DIMS, preferred_element_type=F32)


def _rms(x):
    return x * lax.rsqrt(jnp.mean(x * x, axis=-1, keepdims=True) + RMS_EPS)


def _sigmoid(x):
    return 1.0 / (1.0 + jnp.exp(-x))


def _softplus(x):
    return jnp.maximum(x, 0.0) + jnp.log1p(jnp.exp(-jnp.abs(x)))


def _params(n_axes):
    return pltpu.CompilerParams(dimension_semantics=("arbitrary",) * n_axes,
                                vmem_limit_bytes=VMEM_LIMIT)


def _norm_matmul_kernel(h_ref, g_ref, w_ref, o_ref):
    u = (_rms(h_ref[...]) * g_ref[...]).astype(BF16)
    o_ref[...] = _dot(u, w_ref[...])


def _norm_matmul(h2, g, w, tm):
    t, d = h2.shape
    n = w.shape[1]
    return pl.pallas_call(
        _norm_matmul_kernel,
        grid=(t // tm,),
        in_specs=[pl.BlockSpec((tm, d), lambda i: (i, 0)),
                  pl.BlockSpec((1, d), lambda i: (0, 0)),
                  pl.BlockSpec((d, n), lambda i: (0, 0))],
        out_specs=pl.BlockSpec((tm, n), lambda i: (i, 0)),
        out_shape=jax.ShapeDtypeStruct((t, n), F32),
        compiler_params=_params(1),
        name="norm_matmul",
    )(h2, g.reshape(1, d), w)


def _mlp_kernel(h_ref, g_ref, wu_ref, wd_ref, fg_ref, o_ref, u_scr, acc_scr, *, final):
    j = pl.program_id(1)

    @pl.when(j == 0)
    def _():
        u_scr[...] = (_rms(h_ref[...]) * g_ref[...]).astype(BF16)
        acc_scr[...] = jnp.zeros_like(acc_scr)

    hid = _dot(u_scr[...], wu_ref[...])
    hid = jnp.square(jnp.maximum(hid, 0.0)).astype(BF16)
    acc_scr[...] += _dot(hid, wd_ref[...])

    @pl.when(j == pl.num_programs(1) - 1)
    def _():
        out = h_ref[...] + acc_scr[...]
        if final:
            out = _rms(out) * fg_ref[...]
        o_ref[...] = out


def _mlp(h2, g, wu, wd, fg, final, tm, tf):
    t, d = h2.shape
    ff = wu.shape[1]
    return pl.pallas_call(
        functools.partial(_mlp_kernel, final=final),
        grid=(t // tm, ff // tf),
        in_specs=[pl.BlockSpec((tm, d), lambda i, j: (i, 0)),
                  pl.BlockSpec((1, d), lambda i, j: (0, 0)),
                  pl.BlockSpec((d, tf), lambda i, j: (0, j)),
                  pl.BlockSpec((tf, d), lambda i, j: (j, 0)),
                  pl.BlockSpec((1, d), lambda i, j: (0, 0))],
        out_specs=pl.BlockSpec((tm, d), lambda i, j: (i, 0)),
        out_shape=jax.ShapeDtypeStruct((t, d), F32),
        scratch_shapes=[pltpu.VMEM((tm, d), BF16), pltpu.VMEM((tm, d), F32)],
        compiler_params=_params(2),
        name="mlp",
    )(h2, g.reshape(1, d), wu, wd, fg.reshape(1, d))


def _causal_conv(x, cw, xbuf, tt):
    xbuf[SUBLANES:SUBLANES + tt, :] = x
    out = cw[3:4, :] * x
    for k in range(1, CONV_WIDTH):
        out = out + cw[3 - k:4 - k, :] * xbuf[SUBLANES - k:SUBLANES - k + tt, :]
    xbuf[0:SUBLANES, :] = xbuf[tt:tt + SUBLANES, :]
    return out


def _cumsum_rows(x):
    n = x.shape[0]
    row = lax.broadcasted_iota(jnp.int32, x.shape, 0)
    d = 1
    while d < n:
        x = x + jnp.where(row >= d, pltpu.roll(x, d, 0), 0.0)
        d *= 2
    return x


def _lru_kernel(z_ref, h_ref, cw_ref, cb_ref, wa_ref, ba_ref, wx_ref, bx_ref, lam_ref,
                wout_ref, o_ref, xbuf, carry, hs_scr, *, tt, width):
    @pl.when(pl.program_id(1) == 0)
    def _():
        xbuf[0:SUBLANES, :] = jnp.zeros((SUBLANES, width), F32)
        carry[...] = jnp.zeros_like(carry)

    z = z_ref[0]
    gate = z[:, :width]
    xr = _causal_conv(z[:, width:], cw_ref[...], xbuf, tt) + cb_ref[...]
    xb = xr.astype(BF16)
    nblk = width // LRU_BLOCK
    ra = jnp.concatenate(
        [_dot(xb[:, g * LRU_BLOCK:(g + 1) * LRU_BLOCK], wa_ref[g]) for g in range(nblk)], axis=1)
    ia = jnp.concatenate(
        [_dot(xb[:, g * LRU_BLOCK:(g + 1) * LRU_BLOCK], wx_ref[g]) for g in range(nblk)], axis=1)
    r = _sigmoid(ra + ba_ref[...])
    i = _sigmoid(ia + bx_ref[...])
    log_a = -LRU_C * r * _softplus(-lam_ref[...])
    a = jnp.exp(log_a)
    b = jnp.sqrt(1.0 - a * a) * (i * xr)

    row = lax.broadcasted_iota(jnp.int32, (tt, width), 0) & (SUBLANES - 1)
    d = 1
    while d < SUBLANES:
        keep = row >= d
        b = jnp.where(keep, a * pltpu.roll(b, d, 0) + b, b)
        a = jnp.where(keep, a * pltpu.roll(a, d, 0), a)
        d *= 2
    c = carry[...]
    for g in range(tt // SUBLANES):
        sl = slice(g * SUBLANES, (g + 1) * SUBLANES)
        hg = a[sl, :] * c + b[sl, :]
        hs_scr[sl, :] = hg
        c = jnp.broadcast_to(hg[SUBLANES - 1:SUBLANES, :], (SUBLANES, width))
    carry[...] = c

    y = (jax.nn.gelu(gate) * hs_scr[...]).astype(BF16)
    o_ref[0] = h_ref[0] + _dot(y, wout_ref[...])


def _lru_layer(h, g, w_in, conv_w, conv_b, wa, ba, wx, bx, lam, w_out, tt=256):
    b_, s_, d = h.shape
    width = w_out.shape[0]
    z = _norm_matmul(h.reshape(b_ * s_, d), g, w_in.astype(BF16), 256).reshape(b_, s_, 2 * width)
    row = lambda v: v.reshape(1, width)
    const2 = lambda bi, ti: (0, 0)
    const3 = lambda bi, ti: (0, 0, 0)
    return pl.pallas_call(
        functools.partial(_lru_kernel, tt=tt, width=width),
        grid=(b_, s_ // tt),
        in_specs=[pl.BlockSpec((1, tt, 2 * width), lambda bi, ti: (bi, ti, 0)),
                  pl.BlockSpec((1, tt, d), lambda bi, ti: (bi, ti, 0)),
                  pl.BlockSpec((CONV_WIDTH, width), const2),
                  pl.BlockSpec((1, width), const2),
                  pl.BlockSpec(wa.shape, const3),
                  pl.BlockSpec((1, width), const2),
                  pl.BlockSpec(wx.shape, const3),
                  pl.BlockSpec((1, width), const2),
                  pl.BlockSpec((1, width), const2),
                  pl.BlockSpec((width, d), const2)],
        out_specs=pl.BlockSpec((1, tt, d), lambda bi, ti: (bi, ti, 0)),
        out_shape=jax.ShapeDtypeStruct((b_, s_, d), F32),
        scratch_shapes=[pltpu.VMEM((tt + SUBLANES, width), F32),
                        pltpu.VMEM((SUBLANES, width), F32),
                        pltpu.VMEM((tt, width), F32)],
        compiler_params=_params(2),
        name="lru_mixer",
    )(z, h, conv_w, row(conv_b), wa.astype(BF16), row(ba), wx.astype(BF16), row(bx), row(lam),
      w_out.astype(BF16))


def _rope(x, cos, sin):
    half = x.shape[1] // 2
    x1, x2 = x[:, :half], x[:, half:]
    return jnp.concatenate([x1 * cos - x2 * sin, x2 * cos + x1 * sin], axis=1)


def _ret_kernel(z_ref, h_ref, cos_ref, sin_ref, wout_ref, o_ref, state, y_scr, *, c):
    @pl.when(pl.program_id(1) == 0)
    def _():
        state[...] = jnp.zeros_like(state)

    cos = cos_ref[...]
    sin = sin_ref[...]
    ri = lax.broadcasted_iota(jnp.int32, (c, c), 0)
    ci = lax.broadcasted_iota(jnp.int32, (c, c), 1)
    diff = (ri - ci).astype(F32)
    pos = lax.broadcasted_iota(jnp.int32, (c, 1), 0).astype(F32)
    qk_w = RET_HEADS * RET_DK
    v_w = RET_HEADS * RET_DV
    for hd in range(RET_HEADS):
        log_gamma = math.log1p(-(2.0 ** (-5.0 - hd)))
        q = _rope(z_ref[0, :, hd * RET_DK:(hd + 1) * RET_DK], cos, sin)
        k = _rope(z_ref[0, :, qk_w + hd * RET_DK:qk_w + (hd + 1) * RET_DK], cos, sin) * (RET_DK ** -0.5)
        v = z_ref[0, :, 2 * qk_w + hd * RET_DV:2 * qk_w + (hd + 1) * RET_DV].astype(BF16)
        gt = z_ref[0, :, 2 * qk_w + v_w + hd * RET_DV:2 * qk_w + v_w + (hd + 1) * RET_DV]
        dmask = jnp.where(diff >= 0.0, jnp.exp(log_gamma * jnp.maximum(diff, 0.0)), 0.0)
        inter_dec = jnp.exp(log_gamma * (pos + 1.0))
        k_dec = jnp.exp(log_gamma * (c - 1.0 - pos))
        chunk_dec = math.exp(log_gamma * c)
        qb = q.astype(BF16)
        scores = _dot_nt(qb, k.astype(BF16)) * dmask
        s_old = state[hd]
        o = _dot(scores.astype(BF16), v) + _dot(qb, s_old.astype(BF16)) * inter_dec
        state[hd] = s_old * chunk_dec + _dot_tn((k * k_dec).astype(BF16), v)
        y = gt * _sigmoid(gt) * _rms(o)
        y_scr[:, hd * RET_DV:(hd + 1) * RET_DV] = y.astype(BF16)
    o_ref[0] = h_ref[0] + _dot(y_scr[...], wout_ref[...])


def _ret_layer(h, g, w_in, w_out, cos, sin, c=256):
    b_, s_, d = h.shape
    n_in = w_in.shape[1]
    v_w = w_out.shape[0]
    z = _norm_matmul(h.reshape(b_ * s_, d), g, w_in.astype(BF16), 256).reshape(b_, s_, n_in)
    return pl.pallas_call(
        functools.partial(_ret_kernel, c=c),
        grid=(b_, s_ // c),
        in_specs=[pl.BlockSpec((1, c, n_in), lambda bi, ti: (bi, ti, 0)),
                  pl.BlockSpec((1, c, d), lambda bi, ti: (bi, ti, 0)),
                  pl.BlockSpec((c, RET_DK // 2), lambda bi, ti: (ti, 0)),
                  pl.BlockSpec((c, RET_DK // 2), lambda bi, ti: (ti, 0)),
                  pl.BlockSpec((v_w, d), lambda bi, ti: (0, 0))],
        out_specs=pl.BlockSpec((1, c, d), lambda bi, ti: (bi, ti, 0)),
        out_shape=jax.ShapeDtypeStruct((b_, s_, d), F32),
        scratch_shapes=[pltpu.VMEM((RET_HEADS, RET_DK, RET_DV), F32),
                        pltpu.VMEM((c, v_w), BF16)],
        compiler_params=_params(2),
        name="retention_mixer",
    )(z, h, cos, sin, w_out.astype(BF16))


BATCH_NN = (((2,), (1,)), ((0,), (0,)))
BATCH_NT = (((2,), (2,)), ((0,), (0,)))
BATCH_TN = (((1,), (1,)), ((0,), (0,)))


def _bdot(a, b, dims=BATCH_NN):
    return lax.dot_general(a.astype(BF16), b.astype(BF16), dims, preferred_element_type=F32)


def _unit_lower_inverse(a, ri, ci):
    c = a.shape[1]
    same = lambda sh: (ri >> sh) == (ci >> sh)
    x = jnp.where(same(4), -a, 0.0)
    t = jnp.where(ri == ci, 1.0, 0.0) + x
    p = _bdot(x, x)
    for _ in range(2):
        tp = _bdot(jnp.concatenate([t, p], axis=1), p)
        t = t + tp[:, :c]
        p = tp[:, c:]
    t = t + _bdot(t, p)
    sh = 4
    while (1 << sh) < c:
        low = jnp.where(same(sh + 1) & jnp.logical_not(same(sh)), a, 0.0)
        t = t - _bdot(t, _bdot(low, t))
        sh += 1
    return t


def _gdn_kernel(z_ref, zs_ref, h_ref, cw_ref, alog_ref, dtb_ref, nw_ref, wout_ref, o_ref,
                xbuf, state, y_scr, *, c):
    nq, nv, dh = GDN_QK_HEADS, GDN_V_HEADS, GDN_D
    qk_w = nq * dh
    conv_ch = 2 * qk_w + nv * dh
    rep = nv // nq

    @pl.when(pl.program_id(1) == 0)
    def _():
        xbuf[0:SUBLANES, :] = jnp.zeros((SUBLANES, conv_ch), F32)
        state[...] = jnp.zeros_like(state)

    xc = _causal_conv(z_ref[0, :, :conv_ch], cw_ref[...], xbuf, c)
    qkv = xc * _sigmoid(xc)
    heads = lambda off, n: jnp.stack([qkv[:, off + i * dh:off + (i + 1) * dh] for i in range(n)])
    q = heads(0, nq)
    k = heads(qk_w, nq)
    v = heads(2 * qk_w, nv)
    qn = q * lax.rsqrt(jnp.sum(q * q, axis=-1, keepdims=True) + L2_EPS) * (dh ** -0.5)
    kn = k * lax.rsqrt(jnp.sum(k * k, axis=-1, keepdims=True) + L2_EPS)
    prod = _bdot(jnp.concatenate([kn, qn], axis=1), kn, BATCH_NT)
    per_v = lambda t: jnp.stack([t[i // rep] for i in range(nv)])
    kk = per_v(prod[:, :c])
    qk = per_v(prod[:, c:])
    qn = per_v(qn)
    kn = per_v(kn)

    zs = zs_ref[0]
    beta_all = _sigmoid(zs)
    g_all = -jnp.exp(alog_ref[...]) * _softplus(zs + dtb_ref[...])
    gcum = _cumsum_rows(g_all)
    gcum_t = gcum.T
    beta = jnp.stack([beta_all[:, i:i + 1] for i in range(nv)])
    g_col = jnp.stack([gcum[:, nv + i:nv + i + 1] for i in range(nv)])
    g_row = jnp.stack([gcum_t[nv + i:nv + i + 1, :] for i in range(nv)])
    g_last = g_col[:, c - 1:c, :]
    exp_g = jnp.exp(g_col)

    ri = lax.broadcasted_iota(jnp.int32, (c, c), 0)
    ci = lax.broadcasted_iota(jnp.int32, (c, c), 1)
    decay = jnp.where(ri >= ci, jnp.exp(jnp.minimum(g_col - g_row, 0.0)), 0.0)
    a_mat = jnp.where(ri > ci, beta * kk * decay, 0.0)
    t_inv = _unit_lower_inverse(a_mat, ri, ci)
    uw = _bdot(t_inv, jnp.concatenate([v * beta, kn * (beta * exp_g)], axis=2))
    s_old = state[...]
    wq = _bdot(jnp.concatenate([uw[:, :, dh:], qn * exp_g], axis=1), s_old)
    v_new = uw[:, :, :dh] - wq[:, :c]
    o = wq[:, c:] + _bdot(qk * decay, v_new)
    k_dec = kn * jnp.exp(g_last - g_col)
    state[...] = s_old * jnp.exp(g_last) + _bdot(k_dec, v_new, BATCH_TN)

    o = _rms(o) * nw_ref[...]
    for i in range(nv):
        zg = z_ref[0, :, conv_ch + i * dh:conv_ch + (i + 1) * dh]
        y_scr[:, i * dh:(i + 1) * dh] = (o[i] * (zg * _sigmoid(zg))).astype(BF16)
    o_ref[0] = h_ref[0] + _dot(y_scr[...], wout_ref[...])


def _gdn_layer(h, g, w_in, conv_w, a_log, dt_bias, norm_w, w_out, c=128):
    b_, s_, d = h.shape
    v_w = w_out.shape[0]
    conv_ch = conv_w.shape[1]
    main_w = conv_ch + v_w
    small = w_in.shape[1] - main_w
    w_cat = jnp.concatenate(
        [w_in, jnp.zeros((d, LANES - small), w_in.dtype)], axis=1).astype(BF16)
    n_cat = main_w + LANES
    z = _norm_matmul(h.reshape(b_ * s_, d), g, w_cat, 256).reshape(b_, s_, n_cat)
    pad_row = lambda v: jnp.zeros((1, LANES), F32).at[0, GDN_V_HEADS:2 * GDN_V_HEADS].set(v)
    const2 = lambda bi, ti: (0, 0)
    return pl.pallas_call(
        functools.partial(_gdn_kernel, c=c),
        grid=(b_, s_ // c),
        in_specs=[pl.BlockSpec((1, c, main_w), lambda bi, ti: (bi, ti, 0)),
                  pl.BlockSpec((1, c, LANES), lambda bi, ti: (bi, ti, main_w // LANES)),
                  pl.BlockSpec((1, c, d), lambda bi, ti: (bi, ti, 0)),
                  pl.BlockSpec((CONV_WIDTH, conv_ch), const2),
                  pl.BlockSpec((1, LANES), const2),
                  pl.BlockSpec((1, LANES), const2),
                  pl.BlockSpec((1, GDN_D), const2),
                  pl.BlockSpec((v_w, d), const2)],
        out_specs=pl.BlockSpec((1, c, d), lambda bi, ti: (bi, ti, 0)),
        out_shape=jax.ShapeDtypeStruct((b_, s_, d), F32),
        scratch_shapes=[pltpu.VMEM((c + SUBLANES, conv_ch), F32),
                        pltpu.VMEM((GDN_V_HEADS, GDN_D, GDN_D), F32),
                        pltpu.VMEM((c, v_w), BF16)],
        compiler_params=_params(2),
        name="gdn_mixer",
    )(z, z, h, conv_w, pad_row(a_log), pad_row(dt_bias), norm_w.reshape(1, GDN_D),
      w_out.astype(BF16))


def kernel(x, mix_norm, mlp_norm, mlp_w_up, mlp_w_down, lru_w_in, lru_conv_w, lru_conv_b, lru_wa, lru_ba, lru_wx, lru_bx, lru_lambda, lru_w_out, ret_w_in, ret_w_out, gdn_w_in, gdn_conv_w, gdn_a_log, gdn_dt_bias, gdn_norm, gdn_w_out, final_norm):
    b_, s_, d = x.shape
    depth = mix_norm.shape[0]
    pos = jnp.arange(s_, dtype=F32)
    inv_freq = ROPE_BASE ** (-jnp.arange(0, RET_DK, 2, dtype=F32) / RET_DK)
    ang = pos[:, None] * inv_freq[None, :]
    cos, sin = jnp.cos(ang), jnp.sin(ang)
    h = x
    for i in range(depth):
        kind, j = i % 3, i // 3
        if kind == 0:
            h = _lru_layer(h, mix_norm[i], lru_w_in[j], lru_conv_w[j], lru_conv_b[j], lru_wa[j],
                           lru_ba[j].reshape(-1), lru_wx[j], lru_bx[j].reshape(-1), lru_lambda[j],
                           lru_w_out[j])
        elif kind == 1:
            h = _ret_layer(h, mix_norm[i], ret_w_in[j], ret_w_out[j], cos, sin)
        else:
            h = _gdn_layer(h, mix_norm[i], gdn_w_in[j], gdn_conv_w[j], gdn_a_log[j],
                           gdn_dt_bias[j], gdn_norm[j], gdn_w_out[j])
        h = _mlp(h.reshape(b_ * s_, d), mlp_norm[i], mlp_w_up[i].astype(BF16),
                 mlp_w_down[i].astype(BF16), final_norm, i == depth - 1, 1024, 1024
                 ).reshape(b_, s_, d)
    return h
```

```python
import functools
import math

import jax
import jax.numpy as jnp
from jax import lax
from jax.experimental import pallas as pl
from jax.experimental.pallas import tpu as pltpu

F32 = jnp.float32
BF16 = jnp.bfloat16

RMS_EPS = 1e-6
L2_EPS = 1e-6
CONV_WIDTH = 4
LRU_C = 8.0
LRU_BLOCK = 256
RET_HEADS = 4
RET_DK = 256
RET_DV = 512
ROPE_BASE = 10000.0
GDN_QK_HEADS = 8
GDN_V_HEADS = 16
GDN_D = 128
LANES = 128
SUBLANES = 8
VMEM_LIMIT = 56 * 1024 * 1024

NT_DIMS = (((1,), (1,)), ((), ()))
TN_DIMS = (((0,), (0,)), ((), ()))


def _dot(a, b):
    return jnp.dot(a, b, preferred_element_type=F32)


def _dot_nt(a, b):
    return lax.dot_general(a, b, NT_DIMS, preferred_element_type=F32)


def _dot_tn(a, b):
    return lax.dot_general(a, b, TN_DIMS, preferred_element_type=F32)


def _rms(x):
    return x * lax.rsqrt(jnp.mean(x * x, axis=-1, keepdims=True) + RMS_EPS)


def _sigmoid(x):
    return 0.5 * jnp.tanh(0.5 * x) + 0.5


def _silu(x):
    half = 0.5 * x
    return half * jnp.tanh(half) + half


def _softplus(x):
    return jnp.maximum(x, 0.0) + jnp.log1p(jnp.exp(-jnp.abs(x)))


def _params(n_axes):
    return pltpu.CompilerParams(dimension_semantics=("arbitrary",) * n_axes,
                                vmem_limit_bytes=VMEM_LIMIT)


def _slab_rows(n):
    return SUBLANES * (n + 1)


def _slab_put(slab, x, n):
    for s in range(SUBLANES):
        slab[pl.ds(s * (n + 1), n), :] = x[s * n:(s + 1) * n, :]


def _slab_get(slab, n):
    return jnp.concatenate([slab[pl.ds(s * (n + 1), n), :] for s in range(SUBLANES)], axis=0)


def _slab_group(j, n):
    return pl.ds(j, SUBLANES, stride=n + 1)


def _norm_matmul_kernel(h_ref, g_ref, w_ref, o_ref, *, slab_rows):
    u = (_rms(h_ref[...]) * g_ref[...]).astype(BF16)
    z = _dot(u, w_ref[...])
    if slab_rows is None:
        o_ref[...] = z
    else:
        n = slab_rows // SUBLANES
        for r in range(z.shape[0] // slab_rows):
            for k in range(z.shape[1] // LANES):
                _slab_put(o_ref.at[r, k], z[r * slab_rows:(r + 1) * slab_rows, k * LANES:(k + 1) * LANES], n)
                o_ref[r, k, _slab_group(n, n), :] = jnp.zeros((SUBLANES, LANES), F32)


def _norm_matmul(h2, g, w, tm, slab_rows=None):
    t, d = h2.shape
    n = w.shape[1]
    if slab_rows is None:
        out_spec = pl.BlockSpec((tm, n), lambda i: (i, 0))
        out_shape = jax.ShapeDtypeStruct((t, n), F32)
    else:
        rows = _slab_rows(slab_rows // SUBLANES)
        out_spec = pl.BlockSpec((tm // slab_rows, n // LANES, rows, LANES), lambda i: (i, 0, 0, 0))
        out_shape = jax.ShapeDtypeStruct((t // slab_rows, n // LANES, rows, LANES), F32)
    return pl.pallas_call(
        functools.partial(_norm_matmul_kernel, slab_rows=slab_rows),
        grid=(t // tm,),
        in_specs=[pl.BlockSpec((tm, d), lambda i: (i, 0)),
                  pl.BlockSpec((1, d), lambda i: (0, 0)),
                  pl.BlockSpec((d, n), lambda i: (0, 0))],
        out_specs=out_spec,
        out_shape=out_shape,
        compiler_params=_params(1),
        name="norm_matmul",
    )(h2, g.reshape(1, d), w)


def _mlp_kernel(h_ref, g_ref, wu_ref, wd_ref, fg_ref, o_ref, u_scr, acc_scr, *, final):
    j = pl.program_id(1)

    @pl.when(j == 0)
    def _():
        u_scr[...] = (_rms(h_ref[...]) * g_ref[...]).astype(BF16)
        acc_scr[...] = jnp.zeros_like(acc_scr)

    hid = _dot(u_scr[...], wu_ref[...])
    hid = jnp.square(jnp.maximum(hid, 0.0)).astype(BF16)
    acc_scr[...] += _dot(hid, wd_ref[...])

    @pl.when(j == pl.num_programs(1) - 1)
    def _():
        out = h_ref[...] + acc_scr[...]
        if final:
            out = _rms(out) * fg_ref[...]
        o_ref[...] = out


def _mlp(h2, g, wu, wd, layer, fg, final, tm, tf):
    t, d = h2.shape
    ff = wu.shape[2]
    return pl.pallas_call(
        functools.partial(_mlp_kernel, final=final),
        grid=(t // tm, ff // tf),
        in_specs=[pl.BlockSpec((tm, d), lambda i, j: (i, 0)),
                  pl.BlockSpec((1, d), lambda i, j: (0, 0)),
                  pl.BlockSpec((None, d, tf), lambda i, j: (layer, 0, j)),
                  pl.BlockSpec((None, tf, d), lambda i, j: (layer, j, 0)),
                  pl.BlockSpec((1, d), lambda i, j: (0, 0))],
        out_specs=pl.BlockSpec((tm, d), lambda i, j: (i, 0)),
        out_shape=jax.ShapeDtypeStruct((t, d), F32),
        scratch_shapes=[pltpu.VMEM((tm, d), BF16), pltpu.VMEM((tm, d), F32)],
        compiler_params=_params(2),
        name="mlp",
    )(h2, g.reshape(1, d), wu, wd, fg.reshape(1, d))


def _conv_steps(xs, cw_ref, tail, lanes):
    n = len(xs)
    sub = lax.broadcasted_iota(jnp.int32, xs[0].shape, 0)
    hist = [pltpu.roll(jnp.where(sub == SUBLANES - 1, tail[k - 1, :, lanes], xs[n - k]), 1, 0)
            for k in range(1, CONV_WIDTH)]
    taps = [jnp.broadcast_to(cw_ref[k:k + 1, lanes], xs[0].shape) for k in range(CONV_WIDTH)]
    at = lambda j: xs[j] if j >= 0 else hist[-j - 1]
    out = []
    for j in range(n):
        acc = taps[CONV_WIDTH - 1] * xs[j]
        for k in range(1, CONV_WIDTH):
            acc = acc + taps[CONV_WIDTH - 1 - k] * at(j - k)
        out.append(acc)
    for k in range(1, CONV_WIDTH):
        tail[k - 1, :, lanes] = xs[n - k]
    return out


def _cumsum_rows(x):
    n = x.shape[0]
    row = lax.broadcasted_iota(jnp.int32, x.shape, 0)
    d = 1
    while d < n:
        x = x + jnp.where(row >= d, pltpu.roll(x, d, 0), 0.0)
        d *= 2
    return x


def _lru_kernel(h_ref, g_ref, win_ref, cw_ref, cb_ref, wa_ref, ba_ref, wx_ref, bx_ref, lam_ref,
                wout_ref, o_ref, tail, carry, x_slab, hs_slab, h0_scr, pp_scr, *, tt, width):
    n = tt // SUBLANES
    bw = LRU_BLOCK
    nblk = width // bw
    cpb = bw // LANES
    grp = lambda t, j: t[j * SUBLANES:(j + 1) * SUBLANES]
    grp_ds = lambda j: pl.ds(j * SUBLANES, SUBLANES)

    @pl.when(pl.program_id(1) == 0)
    def _():
        tail[...] = jnp.zeros_like(tail)
        carry[...] = jnp.zeros_like(carry)

    hn = h_ref[0]
    u = (_rms(hn) * g_ref[...]).astype(BF16)
    gate, out = {}, [hn]

    def x_proj(blk):
        zx = _dot(u, win_ref[:, width + blk * bw:width + (blk + 1) * bw])
        for c in range(cpb):
            _slab_put(x_slab.at[blk * cpb + c], zx[:, c * LANES:(c + 1) * LANES], n)

    def gate_proj(blk):
        gate[blk] = _dot(u, win_ref[:, blk * bw:(blk + 1) * bw])

    def recur(blk):
        lanes = slice(blk * bw, (blk + 1) * bw)
        xg = [jnp.concatenate(g, axis=1) for g in zip(*[
            _conv_steps([x_slab[blk * cpb + c, _slab_group(j, n), :] for j in range(n)],
                        cw_ref, tail, slice(blk * bw + c * LANES, blk * bw + (c + 1) * LANES))
            for c in range(cpb)])]
        cb = cb_ref[:, lanes]
        xg = [g + cb for g in xg]
        xb = jnp.concatenate(xg, axis=0).astype(BF16)
        ra = _dot(xb, wa_ref[blk])
        ia = _dot(xb, wx_ref[blk])
        ba = ba_ref[:, lanes]
        bx = bx_ref[:, lanes]
        sp = -LRU_C * _softplus(-lam_ref[:, lanes])
        hc = jnp.zeros((SUBLANES, bw), F32)
        pc = jnp.ones((SUBLANES, bw), F32)
        for j in range(n):
            r = _sigmoid(grp(ra, j) + ba)
            i = _sigmoid(grp(ia, j) + bx)
            aj = jnp.exp(r * sp)
            om = 1.0 - aj * aj
            bj = om * lax.rsqrt(jnp.maximum(om, 1e-30)) * (i * xg[j])
            hc = aj * hc + bj
            pc = aj * pc
            h0_scr[grp_ds(j), lanes] = hc
            pp_scr[grp_ds(j), lanes] = pc
        sub = lax.broadcasted_iota(jnp.int32, (SUBLANES, bw), 0)
        ea, eb = pc, hc
        d = 1
        while d < SUBLANES:
            keep = sub >= d
            eb = jnp.where(keep, ea * pltpu.roll(eb, d, 0) + eb, eb)
            ea = jnp.where(keep, ea * pltpu.roll(ea, d, 0), ea)
            d *= 2
        c0 = carry[:, lanes]
        seg_end = eb + ea * c0
        c_in = jnp.where(sub == 0, c0, pltpu.roll(seg_end, 1, 0))
        carry[:, lanes] = jnp.broadcast_to(seg_end[SUBLANES - 1:SUBLANES, :], (SUBLANES, bw))
        for j in range(n):
            hs = h0_scr[grp_ds(j), lanes] + pp_scr[grp_ds(j), lanes] * c_in
            for c in range(cpb):
                hs_slab[blk * cpb + c, _slab_group(j, n), :] = hs[:, c * LANES:(c + 1) * LANES]

    def out_piece(blk):
        hs_time = jnp.concatenate([_slab_get(hs_slab.at[blk * cpb + c], n) for c in range(cpb)], axis=1)
        y = (jax.nn.gelu(gate[blk]) * hs_time).astype(BF16)
        out[0] = out[0] + _dot(y, wout_ref[blk * bw:(blk + 1) * bw, :])

    x_proj(0)
    for blk in range(nblk):
        if blk + 1 < nblk:
            x_proj(blk + 1)
        gate_proj(blk)
        recur(blk)
        if blk > 0:
            out_piece(blk - 1)
    out_piece(nblk - 1)
    o_ref[0] = out[0]


def _lru_layer(h, g, layer, w_in, wa, wx, w_out, conv_w, conv_b, ba, bx, lam, tt=256):
    b_, s_, d = h.shape
    width = w_out.shape[1]
    row = lambda v: v.reshape(1, width)
    const2 = lambda bi, ti: (0, 0)
    slab = pltpu.VMEM((width // LANES, _slab_rows(tt // SUBLANES), LANES), F32)
    return pl.pallas_call(
        functools.partial(_lru_kernel, tt=tt, width=width),
        grid=(b_, s_ // tt),
        in_specs=[pl.BlockSpec((1, tt, d), lambda bi, ti: (bi, ti, 0)),
                  pl.BlockSpec((1, d), const2),
                  pl.BlockSpec((None, d, 2 * width), lambda bi, ti: (layer, 0, 0)),
                  pl.BlockSpec((CONV_WIDTH, width), const2),
                  pl.BlockSpec((1, width), const2),
                  pl.BlockSpec((None,) + wa.shape[1:], lambda bi, ti: (layer, 0, 0, 0)),
                  pl.BlockSpec((1, width), const2),
                  pl.BlockSpec((None,) + wx.shape[1:], lambda bi, ti: (layer, 0, 0, 0)),
                  pl.BlockSpec((1, width), const2),
                  pl.BlockSpec((1, width), const2),
                  pl.BlockSpec((None, width, d), lambda bi, ti: (layer, 0, 0))],
        out_specs=pl.BlockSpec((1, tt, d), lambda bi, ti: (bi, ti, 0)),
        out_shape=jax.ShapeDtypeStruct((b_, s_, d), F32),
        scratch_shapes=[pltpu.VMEM((CONV_WIDTH - 1, SUBLANES, width), F32),
                        pltpu.VMEM((SUBLANES, width), F32), slab, slab,
                        pltpu.VMEM((tt, width), F32), pltpu.VMEM((tt, width), F32)],
        compiler_params=_params(2),
        name="lru_mixer",
    )(h, g.reshape(1, d), w_in, conv_w, row(conv_b), wa, row(ba), wx, row(bx), row(lam), w_out)


def _rope(x, cos, sin):
    half = x.shape[1] // 2
    x1, x2 = x[:, :half], x[:, half:]
    return jnp.concatenate([x1 * cos - x2 * sin, x2 * cos + x1 * sin], axis=1)


def _ret_kernel(z_ref, h_ref, cos_ref, sin_ref, wout_ref, o_ref, state, y_scr, *, c):
    @pl.when(pl.program_id(1) == 0)
    def _():
        state[...] = jnp.zeros_like(state)

    cos = cos_ref[...]
    sin = sin_ref[...]
    ri = lax.broadcasted_iota(jnp.int32, (c, c), 0)
    ci = lax.broadcasted_iota(jnp.int32, (c, c), 1)
    diff = (ri - ci).astype(F32)
    pos = lax.broadcasted_iota(jnp.int32, (c, 1), 0).astype(F32)
    qk_w = RET_HEADS * RET_DK
    v_w = RET_HEADS * RET_DV
    for hd in range(RET_HEADS):
        log_gamma = math.log1p(-(2.0 ** (-5.0 - hd)))
        q = _rope(z_ref[0, :, hd * RET_DK:(hd + 1) * RET_DK], cos, sin)
        k = _rope(z_ref[0, :, qk_w + hd * RET_DK:qk_w + (hd + 1) * RET_DK], cos, sin) * (RET_DK ** -0.5)
        v = z_ref[0, :, 2 * qk_w + hd * RET_DV:2 * qk_w + (hd + 1) * RET_DV].astype(BF16)
        gt = z_ref[0, :, 2 * qk_w + v_w + hd * RET_DV:2 * qk_w + v_w + (hd + 1) * RET_DV]
        dmask = jnp.where(diff >= 0.0, jnp.exp(log_gamma * jnp.maximum(diff, 0.0)), 0.0)
        inter_dec = jnp.exp(log_gamma * (pos + 1.0))
        k_dec = jnp.exp(log_gamma * (c - 1.0 - pos))
        chunk_dec = math.exp(log_gamma * c)
        qb = q.astype(BF16)
        scores = _dot_nt(qb, k.astype(BF16)) * dmask
        s_old = state[hd]
        o = _dot(scores.astype(BF16), v) + _dot(qb, s_old.astype(BF16)) * inter_dec
        state[hd] = s_old * chunk_dec + _dot_tn((k * k_dec).astype(BF16), v)
        y = _silu(gt) * _rms(o)
        y_scr[:, hd * RET_DV:(hd + 1) * RET_DV] = y.astype(BF16)
    o_ref[0] = h_ref[0] + _dot(y_scr[...], wout_ref[...])


def _ret_layer(h, g, w_in, w_out, cos, sin, c=256):
    b_, s_, d = h.shape
    n_in = w_in.shape[1]
    v_w = w_out.shape[0]
    z = _norm_matmul(h.reshape(b_ * s_, d), g, w_in.astype(BF16), 256).reshape(b_, s_, n_in)
    return pl.pallas_call(
        functools.partial(_ret_kernel, c=c),
        grid=(b_, s_ // c),
        in_specs=[pl.BlockSpec((1, c, n_in), lambda bi, ti: (bi, ti, 0)),
                  pl.BlockSpec((1, c, d), lambda bi, ti: (bi, ti, 0)),
                  pl.BlockSpec((c, RET_DK // 2), lambda bi, ti: (ti, 0)),
                  pl.BlockSpec((c, RET_DK // 2), lambda bi, ti: (ti, 0)),
                  pl.BlockSpec((v_w, d), lambda bi, ti: (0, 0))],
        out_specs=pl.BlockSpec((1, c, d), lambda bi, ti: (bi, ti, 0)),
        out_shape=jax.ShapeDtypeStruct((b_, s_, d), F32),
        scratch_shapes=[pltpu.VMEM((RET_HEADS, RET_DK, RET_DV), F32),
                        pltpu.VMEM((c, v_w), BF16)],
        compiler_params=_params(2),
        name="retention_mixer",
    )(z, h, cos, sin, w_out.astype(BF16))


def _unit_lower_inverse(x0b, f0b, lowb, eye):
    plus_eye = lambda p: jnp.where(eye, 1.0, p).astype(BF16)
    p1 = [_dot(x, x) for x in x0b]
    m1 = [_dot(f0, plus_eye(p)).astype(BF16) for f0, p in zip(f0b, p1)]
    p1b = [p.astype(BF16) for p in p1]
    p2 = [_dot(p, p) for p in p1b]
    p2b = [p.astype(BF16) for p in p2]
    m2 = [_dot(plus_eye(q), plus_eye(_dot(pb, pb))).astype(BF16) for q, pb in zip(p2, p2b)]
    tb = [_dot(u, w).astype(BF16) for u, w in zip(m1, m2)]
    for lows in lowb:
        corr = [jnp.where(eye, 1.0, -_dot(low, t)).astype(BF16) for low, t in zip(lows, tb)]
        tb = [_dot(t, g).astype(BF16) for t, g in zip(tb, corr)]
    return tb


def _gdn_kernel(z_ref, h_ref, cw_ref, alog_ref, dtb_ref, nw_ref, wout_ref, o_ref,
                tail, qkv, qn, kn, state, y_scr, *, c):
    nq, nv, dh = GDN_QK_HEADS, GDN_V_HEADS, GDN_D
    qk_w = nq * dh
    conv_ch = 2 * qk_w + nv * dh
    rep = nv // nq
    n = c // SUBLANES

    @pl.when(pl.program_id(1) == 0)
    def _():
        tail[...] = jnp.zeros_like(tail)
        state[...] = jnp.zeros_like(state)

    for k in range(conv_ch // LANES):
        xs = [z_ref[0, k, _slab_group(j, n), :] for j in range(n)]
        for j, xc in enumerate(_conv_steps(xs, cw_ref, tail, slice(k * LANES, (k + 1) * LANES))):
            qkv[k, _slab_group(j, n), :] = _silu(xc)

    prods = []
    for j in range(nq):
        qj = _slab_get(qkv.at[j], n)
        kj = _slab_get(qkv.at[nq + j], n)
        qn[j] = qj * lax.rsqrt(jnp.sum(qj * qj, axis=-1, keepdims=True) + L2_EPS) * (dh ** -0.5)
        kn[j] = kj * lax.rsqrt(jnp.sum(kj * kj, axis=-1, keepdims=True) + L2_EPS)
        prods.append(_dot_nt(jnp.concatenate([kn[j], qn[j]], axis=0).astype(BF16),
                             kn[j].astype(BF16)))

    zs = _slab_get(z_ref.at[0, (conv_ch + nv * dh) // LANES], n)
    beta_all = _sigmoid(zs)
    gcum = _cumsum_rows(-jnp.exp(alog_ref[...]) * _softplus(zs + dtb_ref[...]))
    gcum_t = gcum.T

    ri = lax.broadcasted_iota(jnp.int32, (c, c), 0)
    ci = lax.broadcasted_iota(jnp.int32, (c, c), 1)
    eye = ri == ci
    strict = ri > ci
    same = lambda sh: (ri >> sh) == (ci >> sh)
    levels = []
    sh = 4
    while (1 << sh) < c:
        levels.append(same(sh + 1) & jnp.logical_not(same(sh)) & strict)
        sh += 1
    diag_blocks = same(4) & strict

    x0b, f0b, attnb, rhsb, q_dec, k_decb, g_tot = [], [], [], [], [], [], []
    lowb = [[] for _ in levels]
    for i in range(nv):
        j = i // rep
        beta = beta_all[:, i:i + 1]
        g_col = gcum[:, nv + i:nv + i + 1]
        g_row = gcum_t[nv + i:nv + i + 1, :]
        g_end = g_col[c - 1:c, :]
        decay = jnp.where(ri >= ci, jnp.exp(jnp.minimum(g_col - g_row, 0.0)), 0.0)
        a_mat = beta * prods[j][:c] * decay
        x0 = jnp.where(diag_blocks, -a_mat, 0.0)
        x0b.append(x0.astype(BF16))
        f0b.append(jnp.where(eye, 1.0, x0).astype(BF16))
        for lvl, mask in enumerate(levels):
            lowb[lvl].append(jnp.where(mask, a_mat, 0.0).astype(BF16))
        attnb.append((prods[j][c:] * decay).astype(BF16))
        exp_g = jnp.exp(g_col)
        v = _slab_get(qkv.at[2 * nq + i], n)
        rhsb.append(jnp.concatenate([v * beta, kn[j] * (beta * exp_g)], axis=1).astype(BF16))
        q_dec.append(qn[j] * exp_g)
        k_decb.append((kn[j] * jnp.exp(g_end - g_col)).astype(BF16))
        g_tot.append(jnp.exp(g_end))

    tb = _unit_lower_inverse(x0b, f0b, lowb, eye)
    uw = [_dot(t, r) for t, r in zip(tb, rhsb)]
    s_old = [state[i] for i in range(nv)]
    wq = [_dot(jnp.concatenate([uw[i][:, dh:], q_dec[i]], axis=0).astype(BF16),
               s_old[i].astype(BF16)) for i in range(nv)]
    v_newb = [(uw[i][:, :dh] - wq[i][:c]).astype(BF16) for i in range(nv)]
    o = [wq[i][c:] + _dot(attnb[i], v_newb[i]) for i in range(nv)]
    for i in range(nv):
        state[i] = s_old[i] * g_tot[i] + _dot_tn(k_decb[i], v_newb[i])
    nw = nw_ref[...]
    for i in range(nv):
        zg = _slab_get(z_ref.at[0, conv_ch // LANES + i], n)
        y_scr[:, i * dh:(i + 1) * dh] = (_rms(o[i]) * nw * _silu(zg)).astype(BF16)
    o_ref[0] = h_ref[0] + _dot(y_scr[...], wout_ref[...])


def _gdn_layer(h, g, w_in, conv_w, a_log, dt_bias, norm_w, w_out, c=128):
    b_, s_, d = h.shape
    v_w = w_out.shape[0]
    conv_ch = conv_w.shape[1]
    main_w = conv_ch + v_w
    small = w_in.shape[1] - main_w
    w_cat = jnp.concatenate(
        [w_in, jnp.zeros((d, LANES - small), w_in.dtype)], axis=1).astype(BF16)
    n_cat = main_w + LANES
    z = _norm_matmul(h.reshape(b_ * s_, d), g, w_cat, 256, slab_rows=c)
    nt = s_ // c
    pad_row = lambda v: jnp.zeros((1, LANES), F32).at[0, GDN_V_HEADS:2 * GDN_V_HEADS].set(v)
    const2 = lambda bi, ti: (0, 0)
    return pl.pallas_call(
        functools.partial(_gdn_kernel, c=c),
        grid=(b_, s_ // c),
        in_specs=[pl.BlockSpec((1, n_cat // LANES, _slab_rows(c // SUBLANES), LANES),
                               lambda bi, ti: (bi * nt + ti, 0, 0, 0)),
                  pl.BlockSpec((1, c, d), lambda bi, ti: (bi, ti, 0)),
                  pl.BlockSpec((CONV_WIDTH, conv_ch), const2),
                  pl.BlockSpec((1, LANES), const2),
                  pl.BlockSpec((1, LANES), const2),
                  pl.BlockSpec((1, GDN_D), const2),
                  pl.BlockSpec((v_w, d), const2)],
        out_specs=pl.BlockSpec((1, c, d), lambda bi, ti: (bi, ti, 0)),
        out_shape=jax.ShapeDtypeStruct((b_, s_, d), F32),
        scratch_shapes=[pltpu.VMEM((CONV_WIDTH - 1, SUBLANES, conv_ch), F32),
                        pltpu.VMEM((conv_ch // LANES, _slab_rows(c // SUBLANES), LANES), F32),
                        pltpu.VMEM((GDN_QK_HEADS, c, GDN_D), F32),
                        pltpu.VMEM((GDN_QK_HEADS, c, GDN_D), F32),
                        pltpu.VMEM((GDN_V_HEADS, GDN_D, GDN_D), F32),
                        pltpu.VMEM((c, v_w), BF16)],
        compiler_params=_params(2),
        name="gdn_mixer",
    )(z, h, conv_w, pad_row(a_log), pad_row(dt_bias), norm_w.reshape(1, GDN_D),
      w_out.astype(BF16))


def kernel(x, mix_norm, mlp_norm, mlp_w_up, mlp_w_down, lru_w_in, lru_conv_w, lru_conv_b, lru_wa, lru_ba, lru_wx, lru_bx, lru_lambda, lru_w_out, ret_w_in, ret_w_out, gdn_w_in, gdn_conv_w, gdn_a_log, gdn_dt_bias, gdn_norm, gdn_w_out, final_norm):
    b_, s_, d = x.shape
    depth = mix_norm.shape[0]
    pos = jnp.arange(s_, dtype=F32)
    inv_freq = ROPE_BASE ** (-jnp.arange(0, RET_DK, 2, dtype=F32) / RET_DK)
    ang = pos[:, None] * inv_freq[None, :]
    cos, sin = jnp.cos(ang), jnp.sin(ang)
    w_up = mlp_w_up.astype(BF16)
    w_down = mlp_w_down.astype(BF16)
    lru_w = [w.astype(BF16) for w in (lru_w_in, lru_wa, lru_wx, lru_w_out)]
    h = x
    for i in range(depth):
        kind, j = i % 3, i // 3
        if kind == 0:
            h = _lru_layer(h, mix_norm[i], j, *lru_w, lru_conv_w[j], lru_conv_b[j],
                           lru_ba[j].reshape(-1), lru_bx[j].reshape(-1), lru_lambda[j])
        elif kind == 1:
            h = _ret_layer(h, mix_norm[i], ret_w_in[j], ret_w_out[j], cos, sin)
        else:
            h = _gdn_layer(h, mix_norm[i], gdn_w_in[j], gdn_conv_w[j], gdn_a_log[j],
                           gdn_dt_bias[j], gdn_norm[j], gdn_w_out[j])
        h = _mlp(h.reshape(b_ * s_, d), mlp_norm[i], w_up, w_down, i, final_norm, i == depth - 1,
                 1024, 1024).reshape(b_, s_, d)
    return h
```

```python
import functools
import itertools
import math

import jax
import jax.numpy as jnp
from jax import lax
from jax.experimental import pallas as pl
from jax.experimental.pallas import tpu as pltpu

F32 = jnp.float32
BF16 = jnp.bfloat16

RMS_EPS = 1e-6
L2_EPS = 1e-6
CONV_WIDTH = 4
LRU_C = 8.0
LRU_BLOCK = 256
RET_HEADS = 4
RET_DK = 256
RET_DV = 512
ROPE_BASE = 10000.0
GDN_QK_HEADS = 8
GDN_V_HEADS = 16
GDN_D = 128
LANES = 128
SUBLANES = 8
VMEM_LIMIT = 56 * 1024 * 1024

NT_DIMS = (((1,), (1,)), ((), ()))
TN_DIMS = (((0,), (0,)), ((), ()))


def _dot(a, b):
    return jnp.dot(a, b, preferred_element_type=F32)


def _dot_nt(a, b):
    return lax.dot_general(a, b, NT_DIMS, preferred_element_type=F32)


def _dot_tn(a, b):
    return lax.dot_general(a, b, TN_DIMS, preferred_element_type=F32)


def _rms(x):
    return x * lax.rsqrt(jnp.mean(x * x, axis=-1, keepdims=True) + RMS_EPS)


def _sigmoid(x):
    return 0.5 * jnp.tanh(0.5 * x) + 0.5


def _silu(x):
    half = 0.5 * x
    return half * jnp.tanh(half) + half


def _softplus(x):
    return jnp.maximum(x, 0.0) + jnp.log1p(jnp.exp(-jnp.abs(x)))


def _params(n_axes):
    return pltpu.CompilerParams(dimension_semantics=("arbitrary",) * n_axes,
                                vmem_limit_bytes=VMEM_LIMIT)


def _slab_rows(n):
    return SUBLANES * (n + 1)


def _slab_put(slab, x, n):
    for s in range(SUBLANES):
        slab[pl.ds(s * (n + 1), n), :] = x[s * n:(s + 1) * n, :]


def _slab_get(slab, n):
    return jnp.concatenate([slab[pl.ds(s * (n + 1), n), :] for s in range(SUBLANES)], axis=0)


def _slab_group(j, n):
    return pl.ds(j, SUBLANES, stride=n + 1)


def _norm_matmul_kernel(h_ref, g_ref, w_ref, o_ref, *, slab_rows):
    u = (_rms(h_ref[...]) * g_ref[...]).astype(BF16)
    z = _dot(u, w_ref[...])
    if slab_rows is None:
        o_ref[...] = z
    else:
        n = slab_rows // SUBLANES
        for r in range(z.shape[0] // slab_rows):
            for k in range(z.shape[1] // LANES):
                _slab_put(o_ref.at[r, k], z[r * slab_rows:(r + 1) * slab_rows, k * LANES:(k + 1) * LANES], n)
                o_ref[r, k, _slab_group(n, n), :] = jnp.zeros((SUBLANES, LANES), F32)


def _norm_matmul(h2, g, w, tm, slab_rows=None):
    t, d = h2.shape
    n = w.shape[1]
    if slab_rows is None:
        out_spec = pl.BlockSpec((tm, n), lambda i: (i, 0))
        out_shape = jax.ShapeDtypeStruct((t, n), F32)
    else:
        rows = _slab_rows(slab_rows // SUBLANES)
        out_spec = pl.BlockSpec((tm // slab_rows, n // LANES, rows, LANES), lambda i: (i, 0, 0, 0))
        out_shape = jax.ShapeDtypeStruct((t // slab_rows, n // LANES, rows, LANES), F32)
    return pl.pallas_call(
        functools.partial(_norm_matmul_kernel, slab_rows=slab_rows),
        grid=(t // tm,),
        in_specs=[pl.BlockSpec((tm, d), lambda i: (i, 0)),
                  pl.BlockSpec((1, d), lambda i: (0, 0)),
                  pl.BlockSpec((d, n), lambda i: (0, 0))],
        out_specs=out_spec,
        out_shape=out_shape,
        compiler_params=_params(1),
        name="norm_matmul",
    )(h2, g.reshape(1, d), w)


def _mlp_kernel(h_ref, g_ref, wu_ref, wd_ref, fg_ref, o_ref, u_scr, acc_scr, *, final):
    j = pl.program_id(1)

    @pl.when(j == 0)
    def _():
        u_scr[...] = (_rms(h_ref[...]) * g_ref[...]).astype(BF16)
        acc_scr[...] = jnp.zeros_like(acc_scr)

    hid = _dot(u_scr[...], wu_ref[...].astype(BF16))
    hid = jnp.square(jnp.maximum(hid, 0.0)).astype(BF16)
    acc_scr[...] += _dot(hid, wd_ref[...].astype(BF16))

    @pl.when(j == pl.num_programs(1) - 1)
    def _():
        out = h_ref[...] + acc_scr[...]
        if final:
            out = _rms(out) * fg_ref[...]
        o_ref[...] = out


def _mlp(h2, g, wu, wd, layer, fg, final, tm, tf):
    t, d = h2.shape
    ff = wu.shape[2]
    return pl.pallas_call(
        functools.partial(_mlp_kernel, final=final),
        grid=(t // tm, ff // tf),
        in_specs=[pl.BlockSpec((tm, d), lambda i, j: (i, 0)),
                  pl.BlockSpec((1, d), lambda i, j: (0, 0)),
                  pl.BlockSpec((None, d, tf), lambda i, j: (layer, 0, j)),
                  pl.BlockSpec((None, tf, d), lambda i, j: (layer, j, 0)),
                  pl.BlockSpec((1, d), lambda i, j: (0, 0))],
        out_specs=pl.BlockSpec((tm, d), lambda i, j: (i, 0)),
        out_shape=jax.ShapeDtypeStruct((t, d), F32),
        scratch_shapes=[pltpu.VMEM((tm, d), BF16), pltpu.VMEM((tm, d), F32)],
        compiler_params=_params(2),
        name="mlp",
    )(h2, g.reshape(1, d), wu, wd, fg.reshape(1, d))


def _conv_steps(xs, cw_ref, tail, lanes):
    n = len(xs)
    sub = lax.broadcasted_iota(jnp.int32, xs[0].shape, 0)
    hist = [pltpu.roll(jnp.where(sub == SUBLANES - 1, tail[k - 1, :, lanes], xs[n - k]), 1, 0)
            for k in range(1, CONV_WIDTH)]
    taps = [jnp.broadcast_to(cw_ref[k:k + 1, lanes], xs[0].shape) for k in range(CONV_WIDTH)]
    at = lambda j: xs[j] if j >= 0 else hist[-j - 1]
    out = []
    for j in range(n):
        acc = taps[CONV_WIDTH - 1] * xs[j]
        for k in range(1, CONV_WIDTH):
            acc = acc + taps[CONV_WIDTH - 1 - k] * at(j - k)
        out.append(acc)
    for k in range(1, CONV_WIDTH):
        tail[k - 1, :, lanes] = xs[n - k]
    return out


def _cumsum_rows(x):
    n = x.shape[0]
    row = lax.broadcasted_iota(jnp.int32, x.shape, 0)
    d = 1
    while d < n:
        x = x + jnp.where(row >= d, pltpu.roll(x, d, 0), 0.0)
        d *= 2
    return x


def _lru_kernel(h_ref, g_ref, win_ref, cw_ref, cb_ref, wa_ref, ba_ref, wx_ref, bx_ref, lam_ref,
                wout_ref, o_ref, tail, carry, x_slab, hs_slab, h0_scr, pp_scr, *, tt, width):
    n = tt // SUBLANES
    bw = LRU_BLOCK
    nblk = width // bw
    cpb = bw // LANES
    grp = lambda t, j: t[j * SUBLANES:(j + 1) * SUBLANES]
    grp_ds = lambda j: pl.ds(j * SUBLANES, SUBLANES)

    @pl.when(pl.program_id(1) == 0)
    def _():
        tail[...] = jnp.zeros_like(tail)
        carry[...] = jnp.zeros_like(carry)

    hn = h_ref[0]
    u = (_rms(hn) * g_ref[...]).astype(BF16)
    gate, out = {}, [hn]

    def x_proj(blk):
        zx = _dot(u, win_ref[:, width + blk * bw:width + (blk + 1) * bw])
        for c in range(cpb):
            _slab_put(x_slab.at[blk * cpb + c], zx[:, c * LANES:(c + 1) * LANES], n)

    def gate_proj(blk):
        gate[blk] = _dot(u, win_ref[:, blk * bw:(blk + 1) * bw])

    def recur(blk):
        lanes = slice(blk * bw, (blk + 1) * bw)
        xg = [jnp.concatenate(g, axis=1) for g in zip(*[
            _conv_steps([x_slab[blk * cpb + c, _slab_group(j, n), :] for j in range(n)],
                        cw_ref, tail, slice(blk * bw + c * LANES, blk * bw + (c + 1) * LANES))
            for c in range(cpb)])]
        cb = cb_ref[:, lanes]
        xg = [g + cb for g in xg]
        xb = jnp.concatenate(xg, axis=0).astype(BF16)
        ra = _dot(xb, wa_ref[blk])
        ia = _dot(xb, wx_ref[blk])
        ba = ba_ref[:, lanes]
        bx = bx_ref[:, lanes]
        sp = -LRU_C * _softplus(-lam_ref[:, lanes])
        hc = jnp.zeros((SUBLANES, bw), F32)
        pc = jnp.ones((SUBLANES, bw), F32)
        for j in range(n):
            r = _sigmoid(grp(ra, j) + ba)
            i = _sigmoid(grp(ia, j) + bx)
            aj = jnp.exp(r * sp)
            om = 1.0 - aj * aj
            bj = om * lax.rsqrt(jnp.maximum(om, 1e-30)) * (i * xg[j])
            hc = aj * hc + bj
            pc = aj * pc
            h0_scr[grp_ds(j), lanes] = hc
            pp_scr[grp_ds(j), lanes] = pc
        sub = lax.broadcasted_iota(jnp.int32, (SUBLANES, bw), 0)
        ea, eb = pc, hc
        d = 1
        while d < SUBLANES:
            keep = sub >= d
            eb = jnp.where(keep, ea * pltpu.roll(eb, d, 0) + eb, eb)
            ea = jnp.where(keep, ea * pltpu.roll(ea, d, 0), ea)
            d *= 2
        c0 = carry[:, lanes]
        seg_end = eb + ea * c0
        c_in = jnp.where(sub == 0, c0, pltpu.roll(seg_end, 1, 0))
        carry[:, lanes] = jnp.broadcast_to(seg_end[SUBLANES - 1:SUBLANES, :], (SUBLANES, bw))
        for j in range(n):
            hs = h0_scr[grp_ds(j), lanes] + pp_scr[grp_ds(j), lanes] * c_in
            for c in range(cpb):
                hs_slab[blk * cpb + c, _slab_group(j, n), :] = hs[:, c * LANES:(c + 1) * LANES]

    def out_piece(blk):
        hs_time = jnp.concatenate([_slab_get(hs_slab.at[blk * cpb + c], n) for c in range(cpb)], axis=1)
        y = (jax.nn.gelu(gate[blk]) * hs_time).astype(BF16)
        out[0] = out[0] + _dot(y, wout_ref[blk * bw:(blk + 1) * bw, :])

    x_proj(0)
    for blk in range(nblk):
        if blk + 1 < nblk:
            x_proj(blk + 1)
        gate_proj(blk)
        recur(blk)
        if blk > 0:
            out_piece(blk - 1)
    out_piece(nblk - 1)
    o_ref[0] = out[0]


def _lru_layer(h, g, layer, w_in, wa, wx, w_out, conv_w, conv_b, ba, bx, lam, tt=256):
    b_, s_, d = h.shape
    width = w_out.shape[1]
    row = lambda v: v.reshape(1, width)
    const2 = lambda bi, ti: (0, 0)
    slab = pltpu.VMEM((width // LANES, _slab_rows(tt // SUBLANES), LANES), F32)
    return pl.pallas_call(
        functools.partial(_lru_kernel, tt=tt, width=width),
        grid=(b_, s_ // tt),
        in_specs=[pl.BlockSpec((1, tt, d), lambda bi, ti: (bi, ti, 0)),
                  pl.BlockSpec((1, d), const2),
                  pl.BlockSpec((None, d, 2 * width), lambda bi, ti: (layer, 0, 0)),
                  pl.BlockSpec((CONV_WIDTH, width), const2),
                  pl.BlockSpec((1, width), const2),
                  pl.BlockSpec((None,) + wa.shape[1:], lambda bi, ti: (layer, 0, 0, 0)),
                  pl.BlockSpec((1, width), const2),
                  pl.BlockSpec((None,) + wx.shape[1:], lambda bi, ti: (layer, 0, 0, 0)),
                  pl.BlockSpec((1, width), const2),
                  pl.BlockSpec((1, width), const2),
                  pl.BlockSpec((None, width, d), lambda bi, ti: (layer, 0, 0))],
        out_specs=pl.BlockSpec((1, tt, d), lambda bi, ti: (bi, ti, 0)),
        out_shape=jax.ShapeDtypeStruct((b_, s_, d), F32),
        scratch_shapes=[pltpu.VMEM((CONV_WIDTH - 1, SUBLANES, width), F32),
                        pltpu.VMEM((SUBLANES, width), F32), slab, slab,
                        pltpu.VMEM((tt, width), F32), pltpu.VMEM((tt, width), F32)],
        compiler_params=_params(2),
        name="lru_mixer",
    )(h, g.reshape(1, d), w_in, conv_w, row(conv_b), wa, row(ba), wx, row(bx), row(lam), w_out)


def _rope(x, cos, sin):
    half = x.shape[1] // 2
    x1, x2 = x[:, :half], x[:, half:]
    return jnp.concatenate([x1 * cos - x2 * sin, x2 * cos + x1 * sin], axis=1)


def _ret_kernel(z_ref, h_ref, cos_ref, sin_ref, wout_ref, o_ref, state, y_scr, *, c):
    @pl.when(pl.program_id(1) == 0)
    def _():
        state[...] = jnp.zeros_like(state)

    cos = cos_ref[...]
    sin = sin_ref[...]
    ri = lax.broadcasted_iota(jnp.int32, (c, c), 0)
    ci = lax.broadcasted_iota(jnp.int32, (c, c), 1)
    diff = (ri - ci).astype(F32)
    pos = lax.broadcasted_iota(jnp.int32, (c, 1), 0).astype(F32)
    qk_w = RET_HEADS * RET_DK
    v_w = RET_HEADS * RET_DV
    for hd in range(RET_HEADS):
        log_gamma = math.log1p(-(2.0 ** (-5.0 - hd)))
        q = _rope(z_ref[0, :, hd * RET_DK:(hd + 1) * RET_DK], cos, sin)
        k = _rope(z_ref[0, :, qk_w + hd * RET_DK:qk_w + (hd + 1) * RET_DK], cos, sin) * (RET_DK ** -0.5)
        v = z_ref[0, :, 2 * qk_w + hd * RET_DV:2 * qk_w + (hd + 1) * RET_DV].astype(BF16)
        gt = z_ref[0, :, 2 * qk_w + v_w + hd * RET_DV:2 * qk_w + v_w + (hd + 1) * RET_DV]
        dmask = jnp.where(diff >= 0.0, jnp.exp(log_gamma * jnp.maximum(diff, 0.0)), 0.0)
        inter_dec = jnp.exp(log_gamma * (pos + 1.0))
        k_dec = jnp.exp(log_gamma * (c - 1.0 - pos))
        chunk_dec = math.exp(log_gamma * c)
        qb = q.astype(BF16)
        scores = _dot_nt(qb, k.astype(BF16)) * dmask
        s_old = state[hd]
        o = _dot(scores.astype(BF16), v) + _dot(qb, s_old.astype(BF16)) * inter_dec
        state[hd] = s_old * chunk_dec + _dot_tn((k * k_dec).astype(BF16), v)
        y = _silu(gt) * _rms(o)
        y_scr[:, hd * RET_DV:(hd + 1) * RET_DV] = y.astype(BF16)
    o_ref[0] = h_ref[0] + _dot(y_scr[...], wout_ref[...])


def _ret_layer(h, g, w_in, w_out, cos, sin, c=256):
    b_, s_, d = h.shape
    n_in = w_in.shape[1]
    v_w = w_out.shape[0]
    z = _norm_matmul(h.reshape(b_ * s_, d), g, w_in.astype(BF16), 256).reshape(b_, s_, n_in)
    return pl.pallas_call(
        functools.partial(_ret_kernel, c=c),
        grid=(b_, s_ // c),
        in_specs=[pl.BlockSpec((1, c, n_in), lambda bi, ti: (bi, ti, 0)),
                  pl.BlockSpec((1, c, d), lambda bi, ti: (bi, ti, 0)),
                  pl.BlockSpec((c, RET_DK // 2), lambda bi, ti: (ti, 0)),
                  pl.BlockSpec((c, RET_DK // 2), lambda bi, ti: (ti, 0)),
                  pl.BlockSpec((v_w, d), lambda bi, ti: (0, 0))],
        out_specs=pl.BlockSpec((1, c, d), lambda bi, ti: (bi, ti, 0)),
        out_shape=jax.ShapeDtypeStruct((b_, s_, d), F32),
        scratch_shapes=[pltpu.VMEM((RET_HEADS, RET_DK, RET_DV), F32),
                        pltpu.VMEM((c, v_w), BF16)],
        compiler_params=_params(2),
        name="retention_mixer",
    )(z, h, cos, sin, w_out.astype(BF16))


GDN_CHUNKS_PER_STEP = 2
GDN_SIDE_UNITS = 3


def _interleave(main, side, per_step):
    for _ in main:
        for _ in range(per_step):
            next(side, None)
    for _ in side:
        pass


def _gdn_kernel(z_ref, h_ref, cw_ref, alog_ref, dtb_ref, nw_ref, wout_ref, o_ref,
                tail, qkv, qn, kn, state, y_scr, *, c, nchunk):
    nq, nv, dh = GDN_QK_HEADS, GDN_V_HEADS, GDN_D
    qk_w = nq * dh
    conv_ch = 2 * qk_w + nv * dh
    rep = nv // nq
    n = c // SUBLANES

    @pl.when(pl.program_id(1) == 0)
    def _():
        tail[...] = jnp.zeros_like(tail)
        state[...] = jnp.zeros_like(state)

    ri = lax.broadcasted_iota(jnp.int32, (c, c), 0)
    ci = lax.broadcasted_iota(jnp.int32, (c, c), 1)
    eye = ri == ci
    strict = ri > ci
    same = lambda sh: (ri >> sh) == (ci >> sh)
    levels = []
    sh = 4
    while (1 << sh) < c:
        levels.append(same(sh + 1) & jnp.logical_not(same(sh)) & strict)
        sh += 1
    diag_blocks = same(4) & strict
    plus_eye = lambda p: jnp.where(eye, 1.0, p).astype(BF16)
    heads = range(nv)

    def prep(q):
        ctx = dict(prods=[], x0b=[], f0b=[], attnb=[], rhsb=[], q_dec=[], k_decb=[], g_tot=[],
                   lowb=[[] for _ in levels])
        for k in range(conv_ch // LANES):
            xs = [z_ref[q, k, _slab_group(j, n), :] for j in range(n)]
            for j, xc in enumerate(_conv_steps(xs, cw_ref, tail, slice(k * LANES, (k + 1) * LANES))):
                qkv[q, k, _slab_group(j, n), :] = _silu(xc)
            yield
        zs = _slab_get(z_ref.at[q, (conv_ch + nv * dh) // LANES], n)
        beta_all = _sigmoid(zs)
        gcum = _cumsum_rows(-jnp.exp(alog_ref[...]) * _softplus(zs + dtb_ref[...]))
        gcum_t = gcum.T
        yield
        for j in range(nq):
            qj = _slab_get(qkv.at[q, j], n)
            kj = _slab_get(qkv.at[q, nq + j], n)
            qn[q, j] = qj * lax.rsqrt(jnp.sum(qj * qj, axis=-1, keepdims=True) + L2_EPS) * (dh ** -0.5)
            kn[q, j] = kj * lax.rsqrt(jnp.sum(kj * kj, axis=-1, keepdims=True) + L2_EPS)
            ctx["prods"].append(_dot_nt(jnp.concatenate([kn[q, j], qn[q, j]], axis=0).astype(BF16),
                                        kn[q, j].astype(BF16)))
            yield
        for i in heads:
            j = i // rep
            beta = beta_all[:, i:i + 1]
            g_col = gcum[:, nv + i:nv + i + 1]
            g_row = gcum_t[nv + i:nv + i + 1, :]
            g_end = g_col[c - 1:c, :]
            decay = jnp.where(ri >= ci, jnp.exp(jnp.minimum(g_col - g_row, 0.0)), 0.0)
            a_mat = beta * ctx["prods"][j][:c] * decay
            x0 = jnp.where(diag_blocks, -a_mat, 0.0)
            ctx["x0b"].append(x0.astype(BF16))
            ctx["f0b"].append(jnp.where(eye, 1.0, x0).astype(BF16))
            for lvl, mask in enumerate(levels):
                ctx["lowb"][lvl].append(jnp.where(mask, a_mat, 0.0).astype(BF16))
            ctx["attnb"].append((ctx["prods"][j][c:] * decay).astype(BF16))
            exp_g = jnp.exp(g_col)
            v = _slab_get(qkv.at[q, 2 * nq + i], n)
            ctx["rhsb"].append(jnp.concatenate([v * beta, kn[q, j] * (beta * exp_g)], axis=1).astype(BF16))
            ctx["q_dec"].append(qn[q, j] * exp_g)
            ctx["k_decb"].append((kn[q, j] * jnp.exp(g_end - g_col)).astype(BF16))
            ctx["g_tot"].append(jnp.exp(g_end))
            yield
        ctxs[q] = ctx

    def mxu(q):
        cx = ctxs[q]
        p1 = [_dot(x, x) for x in cx["x0b"]]
        yield
        m1 = [_dot(f0, plus_eye(p)).astype(BF16) for f0, p in zip(cx["f0b"], p1)]
        p1b = [p.astype(BF16) for p in p1]
        yield
        p2 = [_dot(p, p) for p in p1b]
        yield
        p2b = [p.astype(BF16) for p in p2]
        p3 = [_dot(pb, pb) for pb in p2b]
        yield
        m2 = [_dot(plus_eye(a), plus_eye(b_)).astype(BF16) for a, b_ in zip(p2, p3)]
        yield
        tb = [_dot(u, w).astype(BF16) for u, w in zip(m1, m2)]
        yield
        for lows in cx["lowb"]:
            corr = [jnp.where(eye, 1.0, -_dot(low, t)).astype(BF16) for low, t in zip(lows, tb)]
            yield
            tb = [_dot(t, g).astype(BF16) for t, g in zip(tb, corr)]
            yield
        uw = [_dot(t, r) for t, r in zip(tb, cx["rhsb"])]
        yield
        s_old = [state[i] for i in heads]
        wq = [_dot(jnp.concatenate([uw[i][:, dh:], cx["q_dec"][i]], axis=0).astype(BF16),
                   s_old[i].astype(BF16)) for i in heads]
        yield
        v_newb = [(uw[i][:, :dh] - wq[i][:c]).astype(BF16) for i in heads]
        cx["o"] = [wq[i][c:] + _dot(cx["attnb"][i], v_newb[i]) for i in heads]
        yield
        for i in heads:
            state[i] = s_old[i] * cx["g_tot"][i] + _dot_tn(cx["k_decb"][i], v_newb[i])
        yield

    def post(q):
        nw = nw_ref[...]
        for i in heads:
            zg = _slab_get(z_ref.at[q, conv_ch // LANES + i], n)
            y_scr[q * c:(q + 1) * c, i * dh:(i + 1) * dh] = (_rms(ctxs[q]["o"][i]) * nw * _silu(zg)).astype(BF16)
            yield

    ctxs = {}
    for _ in prep(0):
        pass
    for q in range(nchunk):
        side = itertools.chain(post(q - 1) if q > 0 else (), prep(q + 1) if q + 1 < nchunk else ())
        _interleave(mxu(q), side, GDN_SIDE_UNITS)
    for _ in post(nchunk - 1):
        pass
    o_ref[0] = h_ref[0] + _dot(y_scr[...], wout_ref[...])


def _gdn_layer(h, g, w_in, conv_w, a_log, dt_bias, norm_w, w_out, c=128, nchunk=GDN_CHUNKS_PER_STEP):
    b_, s_, d = h.shape
    v_w = w_out.shape[0]
    conv_ch = conv_w.shape[1]
    main_w = conv_ch + v_w
    small = w_in.shape[1] - main_w
    w_cat = jnp.concatenate(
        [w_in, jnp.zeros((d, LANES - small), w_in.dtype)], axis=1).astype(BF16)
    n_cat = main_w + LANES
    z = _norm_matmul(h.reshape(b_ * s_, d), g, w_cat, 256, slab_rows=c)
    tt = c * nchunk
    nt = s_ // tt
    pad_row = lambda v: jnp.zeros((1, LANES), F32).at[0, GDN_V_HEADS:2 * GDN_V_HEADS].set(v)
    const2 = lambda bi, ti: (0, 0)
    rows = _slab_rows(c // SUBLANES)
    return pl.pallas_call(
        functools.partial(_gdn_kernel, c=c, nchunk=nchunk),
        grid=(b_, nt),
        in_specs=[pl.BlockSpec((nchunk, n_cat // LANES, rows, LANES), lambda bi, ti: (bi * nt + ti, 0, 0, 0)),
                  pl.BlockSpec((1, tt, d), lambda bi, ti: (bi, ti, 0)),
                  pl.BlockSpec((CONV_WIDTH, conv_ch), const2),
                  pl.BlockSpec((1, LANES), const2),
                  pl.BlockSpec((1, LANES), const2),
                  pl.BlockSpec((1, GDN_D), const2),
                  pl.BlockSpec((v_w, d), const2)],
        out_specs=pl.BlockSpec((1, tt, d), lambda bi, ti: (bi, ti, 0)),
        out_shape=jax.ShapeDtypeStruct((b_, s_, d), F32),
        scratch_shapes=[pltpu.VMEM((CONV_WIDTH - 1, SUBLANES, conv_ch), F32),
                        pltpu.VMEM((nchunk, conv_ch // LANES, rows, LANES), F32),
                        pltpu.VMEM((nchunk, GDN_QK_HEADS, c, GDN_D), F32),
                        pltpu.VMEM((nchunk, GDN_QK_HEADS, c, GDN_D), F32),
                        pltpu.VMEM((GDN_V_HEADS, GDN_D, GDN_D), F32),
                        pltpu.VMEM((tt, v_w), BF16)],
        compiler_params=_params(2),
        name="gdn_mixer",
    )(z, h, conv_w, pad_row(a_log), pad_row(dt_bias), norm_w.reshape(1, GDN_D),
      w_out.astype(BF16))


def kernel(x, mix_norm, mlp_norm, mlp_w_up, mlp_w_down, lru_w_in, lru_conv_w, lru_conv_b, lru_wa, lru_ba, lru_wx, lru_bx, lru_lambda, lru_w_out, ret_w_in, ret_w_out, gdn_w_in, gdn_conv_w, gdn_a_log, gdn_dt_bias, gdn_norm, gdn_w_out, final_norm):
    b_, s_, d = x.shape
    depth = mix_norm.shape[0]
    pos = jnp.arange(s_, dtype=F32)
    inv_freq = ROPE_BASE ** (-jnp.arange(0, RET_DK, 2, dtype=F32) / RET_DK)
    ang = pos[:, None] * inv_freq[None, :]
    cos, sin = jnp.cos(ang), jnp.sin(ang)
    lru_w = [w.astype(BF16) for w in (lru_w_in, lru_wa, lru_wx, lru_w_out)]
    h = x
    for i in range(depth):
        kind, j = i % 3, i // 3
        if kind == 0:
            h = _lru_layer(h, mix_norm[i], j, *lru_w, lru_conv_w[j], lru_conv_b[j],
                           lru_ba[j].reshape(-1), lru_bx[j].reshape(-1), lru_lambda[j])
        elif kind == 1:
            h = _ret_layer(h, mix_norm[i], ret_w_in[j], ret_w_out[j], cos, sin)
        else:
            h = _gdn_layer(h, mix_norm[i], gdn_w_in[j], gdn_conv_w[j], gdn_a_log[j],
                           gdn_dt_bias[j], gdn_norm[j], gdn_w_out[j])
        h = _mlp(h.reshape(b_ * s_, d), mlp_norm[i], mlp_w_up, mlp_w_down, i, final_norm, i == depth - 1,
                 1024, 1024).reshape(b_, s_, d)
    return h
```

```python
import functools
import itertools
import math

import jax
import jax.numpy as jnp
from jax import lax
from jax.experimental import pallas as pl
from jax.experimental.pallas import tpu as pltpu

F32 = jnp.float32
BF16 = jnp.bfloat16

RMS_EPS = 1e-6
L2_EPS = 1e-6
CONV_WIDTH = 4
LRU_C = 8.0
LRU_BLOCK = 256
RET_HEADS = 4
RET_DK = 256
RET_DV = 512
ROPE_BASE = 10000.0
GDN_QK_HEADS = 8
GDN_V_HEADS = 16
GDN_D = 128
LANES = 128
SUBLANES = 8
VMEM_LIMIT = 56 * 1024 * 1024

NT_DIMS = (((1,), (1,)), ((), ()))
TN_DIMS = (((0,), (0,)), ((), ()))


def _dot(a, b):
    return jnp.dot(a, b, preferred_element_type=F32)


def _dot_nt(a, b):
    return lax.dot_general(a, b, NT_DIMS, preferred_element_type=F32)


def _dot_tn(a, b):
    return lax.dot_general(a, b, TN_DIMS, preferred_element_type=F32)


def _rms(x):
    return x * lax.rsqrt(jnp.mean(x * x, axis=-1, keepdims=True) + RMS_EPS)


def _sigmoid(x):
    return 0.5 * jnp.tanh(0.5 * x) + 0.5


def _silu(x):
    half = 0.5 * x
    return half * jnp.tanh(half) + half


def _softplus(x):
    return jnp.maximum(x, 0.0) + jnp.log1p(jnp.exp(-jnp.abs(x)))


def _params(n_axes):
    return pltpu.CompilerParams(dimension_semantics=("arbitrary",) * n_axes,
                                vmem_limit_bytes=VMEM_LIMIT)


def _slab_rows(n):
    return SUBLANES * (n + 1)


def _slab_put(slab, x, n):
    for s in range(SUBLANES):
        slab[pl.ds(s * (n + 1), n), :] = x[s * n:(s + 1) * n, :]


def _slab_get(slab, n):
    return jnp.concatenate([slab[pl.ds(s * (n + 1), n), :] for s in range(SUBLANES)], axis=0)


def _slab_group(j, n):
    return pl.ds(j, SUBLANES, stride=n + 1)


def _norm_matmul_kernel(h_ref, g_ref, w_ref, *rest, slab_rows):
    o_ref = rest[-1]
    u = (_rms(h_ref[...]) * g_ref[...]).astype(BF16)
    z = _dot(u, w_ref[...])
    if slab_rows is None:
        o_ref[...] = z
    else:
        n = slab_rows // SUBLANES
        z2 = _dot(u, rest[0][...])
        nk = z.shape[1] // LANES
        for r in range(z.shape[0] // slab_rows):
            rows = slice(r * slab_rows, (r + 1) * slab_rows)
            for k in range(nk + 1):
                piece = z[rows, k * LANES:(k + 1) * LANES] if k < nk else z2[rows, :]
                _slab_put(o_ref.at[r, k], piece, n)
                o_ref[r, k, _slab_group(n, n), :] = jnp.zeros((SUBLANES, LANES), F32)


def _norm_matmul(h2, g, w, tm, slab_rows=None, w_tail=None):
    t, d = h2.shape
    n = w.shape[1]
    in_specs = [pl.BlockSpec((tm, d), lambda i: (i, 0)),
                pl.BlockSpec((1, d), lambda i: (0, 0)),
                pl.BlockSpec((d, n), lambda i: (0, 0), pipeline_mode=pl.Buffered(1))]
    args = [h2, g.reshape(1, d), w]
    if slab_rows is None:
        out_spec = pl.BlockSpec((tm, n), lambda i: (i, 0))
        out_shape = jax.ShapeDtypeStruct((t, n), F32)
    else:
        rows = _slab_rows(slab_rows // SUBLANES)
        nslab = n // LANES + 1
        in_specs.append(pl.BlockSpec((d, LANES), lambda i: (0, 0), pipeline_mode=pl.Buffered(1)))
        args.append(w_tail)
        out_spec = pl.BlockSpec((tm // slab_rows, nslab, rows, LANES), lambda i: (i, 0, 0, 0))
        out_shape = jax.ShapeDtypeStruct((t // slab_rows, nslab, rows, LANES), F32)
    return pl.pallas_call(
        functools.partial(_norm_matmul_kernel, slab_rows=slab_rows),
        grid=(t // tm,),
        in_specs=in_specs,
        out_specs=out_spec,
        out_shape=out_shape,
        compiler_params=_params(1),
        name="norm_matmul",
    )(*args)


def _mlp_kernel(h_ref, g_ref, wu_ref, wd_ref, fg_ref, o_ref, u_scr, acc_scr, *, final):
    j = pl.program_id(1)

    @pl.when(j == 0)
    def _():
        u_scr[...] = (_rms(h_ref[...]) * g_ref[...]).astype(BF16)
        acc_scr[...] = jnp.zeros_like(acc_scr)

    hid = _dot(u_scr[...], wu_ref[...].astype(BF16))
    hid = jnp.square(jnp.maximum(hid, 0.0)).astype(BF16)
    acc_scr[...] += _dot(hid, wd_ref[...].astype(BF16))

    @pl.when(j == pl.num_programs(1) - 1)
    def _():
        out = h_ref[...] + acc_scr[...]
        if final:
            out = _rms(out) * fg_ref[...]
        o_ref[...] = out


def _mlp(h2, g, wu, wd, layer, fg, final, tm, tf):
    t, d = h2.shape
    ff = wu.shape[2]
    return pl.pallas_call(
        functools.partial(_mlp_kernel, final=final),
        grid=(t // tm, ff // tf),
        in_specs=[pl.BlockSpec((tm, d), lambda i, j: (i, 0)),
                  pl.BlockSpec((1, d), lambda i, j: (0, 0)),
                  pl.BlockSpec((None, d, tf), lambda i, j: (layer, 0, j)),
                  pl.BlockSpec((None, tf, d), lambda i, j: (layer, j, 0)),
                  pl.BlockSpec((1, d), lambda i, j: (0, 0))],
        out_specs=pl.BlockSpec((tm, d), lambda i, j: (i, 0)),
        out_shape=jax.ShapeDtypeStruct((t, d), F32),
        scratch_shapes=[pltpu.VMEM((tm, d), BF16), pltpu.VMEM((tm, d), F32)],
        compiler_params=_params(2),
        name="mlp",
    )(h2, g.reshape(1, d), wu, wd, fg.reshape(1, d))


def _conv_steps(xs, cw_ref, tail, lanes):
    n = len(xs)
    sub = lax.broadcasted_iota(jnp.int32, xs[0].shape, 0)
    hist = [pltpu.roll(jnp.where(sub == SUBLANES - 1, tail[k - 1, :, lanes], xs[n - k]), 1, 0)
            for k in range(1, CONV_WIDTH)]
    taps = [jnp.broadcast_to(cw_ref[k:k + 1, lanes], xs[0].shape) for k in range(CONV_WIDTH)]
    at = lambda j: xs[j] if j >= 0 else hist[-j - 1]
    out = []
    for j in range(n):
        acc = taps[CONV_WIDTH - 1] * xs[j]
        for k in range(1, CONV_WIDTH):
            acc = acc + taps[CONV_WIDTH - 1 - k] * at(j - k)
        out.append(acc)
    for k in range(1, CONV_WIDTH):
        tail[k - 1, :, lanes] = xs[n - k]
    return out


def _cumsum_rows(x):
    n = x.shape[0]
    row = lax.broadcasted_iota(jnp.int32, x.shape, 0)
    d = 1
    while d < n:
        x = x + jnp.where(row >= d, pltpu.roll(x, d, 0), 0.0)
        d *= 2
    return x


def _lru_kernel(h_ref, g_ref, win_ref, cw_ref, cb_ref, wa_ref, ba_ref, wx_ref, bx_ref, lam_ref,
                wout_ref, o_ref, tail, carry, x_slab, hs_slab, h0_scr, pp_scr, *, tt, width):
    n = tt // SUBLANES
    bw = LRU_BLOCK
    nblk = width // bw
    cpb = bw // LANES
    grp = lambda t, j: t[j * SUBLANES:(j + 1) * SUBLANES]
    grp_ds = lambda j: pl.ds(j * SUBLANES, SUBLANES)

    @pl.when(pl.program_id(1) == 0)
    def _():
        tail[...] = jnp.zeros_like(tail)
        carry[...] = jnp.zeros_like(carry)

    hn = h_ref[0]
    u = (_rms(hn) * g_ref[...]).astype(BF16)
    gate, out = {}, [hn]

    def x_proj(blk):
        zx = _dot(u, win_ref[:, width + blk * bw:width + (blk + 1) * bw])
        for c in range(cpb):
            _slab_put(x_slab.at[blk * cpb + c], zx[:, c * LANES:(c + 1) * LANES], n)

    def gate_proj(blk):
        gate[blk] = _dot(u, win_ref[:, blk * bw:(blk + 1) * bw])

    def recur(blk):
        lanes = slice(blk * bw, (blk + 1) * bw)
        xg = [jnp.concatenate(g, axis=1) for g in zip(*[
            _conv_steps([x_slab[blk * cpb + c, _slab_group(j, n), :] for j in range(n)],
                        cw_ref, tail, slice(blk * bw + c * LANES, blk * bw + (c + 1) * LANES))
            for c in range(cpb)])]
        cb = cb_ref[:, lanes]
        xg = [g + cb for g in xg]
        xb = jnp.concatenate(xg, axis=0).astype(BF16)
        ra = _dot(xb, wa_ref[blk])
        ia = _dot(xb, wx_ref[blk])
        ba = ba_ref[:, lanes]
        bx = bx_ref[:, lanes]
        sp = -LRU_C * _softplus(-lam_ref[:, lanes])
        hc = jnp.zeros((SUBLANES, bw), F32)
        pc = jnp.ones((SUBLANES, bw), F32)
        for j in range(n):
            r = _sigmoid(grp(ra, j) + ba)
            i = _sigmoid(grp(ia, j) + bx)
            aj = jnp.exp(r * sp)
            om = 1.0 - aj * aj
            bj = om * lax.rsqrt(jnp.maximum(om, 1e-30)) * (i * xg[j])
            hc = aj * hc + bj
            pc = aj * pc
            h0_scr[grp_ds(j), lanes] = hc
            pp_scr[grp_ds(j), lanes] = pc
        sub = lax.broadcasted_iota(jnp.int32, (SUBLANES, bw), 0)
        ea, eb = pc, hc
        d = 1
        while d < SUBLANES:
            keep = sub >= d
            eb = jnp.where(keep, ea * pltpu.roll(eb, d, 0) + eb, eb)
            ea = jnp.where(keep, ea * pltpu.roll(ea, d, 0), ea)
            d *= 2
        c0 = carry[:, lanes]
        seg_end = eb + ea * c0
        c_in = jnp.where(sub == 0, c0, pltpu.roll(seg_end, 1, 0))
        carry[:, lanes] = jnp.broadcast_to(seg_end[SUBLANES - 1:SUBLANES, :], (SUBLANES, bw))
        for j in range(n):
            hs = h0_scr[grp_ds(j), lanes] + pp_scr[grp_ds(j), lanes] * c_in
            for c in range(cpb):
                hs_slab[blk * cpb + c, _slab_group(j, n), :] = hs[:, c * LANES:(c + 1) * LANES]

    def out_piece(blk):
        hs_time = jnp.concatenate([_slab_get(hs_slab.at[blk * cpb + c], n) for c in range(cpb)], axis=1)
        y = (jax.nn.gelu(gate[blk]) * hs_time).astype(BF16)
        out[0] = out[0] + _dot(y, wout_ref[blk * bw:(blk + 1) * bw, :])

    x_proj(0)
    for blk in range(nblk):
        if blk + 1 < nblk:
            x_proj(blk + 1)
        gate_proj(blk)
        recur(blk)
        if blk > 0:
            out_piece(blk - 1)
    out_piece(nblk - 1)
    o_ref[0] = out[0]


def _lru_layer(h, g, layer, w_in, wa, wx, w_out, conv_w, conv_b, ba, bx, lam, tt=512):
    b_, s_, d = h.shape
    width = w_out.shape[1]
    row = lambda v: v.reshape(1, width)
    const2 = lambda bi, ti: (0, 0)
    slab = pltpu.VMEM((width // LANES, _slab_rows(tt // SUBLANES), LANES), F32)
    return pl.pallas_call(
        functools.partial(_lru_kernel, tt=tt, width=width),
        grid=(b_, s_ // tt),
        in_specs=[pl.BlockSpec((1, tt, d), lambda bi, ti: (bi, ti, 0)),
                  pl.BlockSpec((1, d), const2),
                  pl.BlockSpec((None, d, 2 * width), lambda bi, ti: (layer, 0, 0)),
                  pl.BlockSpec((CONV_WIDTH, width), const2),
                  pl.BlockSpec((1, width), const2),
                  pl.BlockSpec((None,) + wa.shape[1:], lambda bi, ti: (layer, 0, 0, 0)),
                  pl.BlockSpec((1, width), const2),
                  pl.BlockSpec((None,) + wx.shape[1:], lambda bi, ti: (layer, 0, 0, 0)),
                  pl.BlockSpec((1, width), const2),
                  pl.BlockSpec((1, width), const2),
                  pl.BlockSpec((None, width, d), lambda bi, ti: (layer, 0, 0))],
        out_specs=pl.BlockSpec((1, tt, d), lambda bi, ti: (bi, ti, 0)),
        out_shape=jax.ShapeDtypeStruct((b_, s_, d), F32),
        scratch_shapes=[pltpu.VMEM((CONV_WIDTH - 1, SUBLANES, width), F32),
                        pltpu.VMEM((SUBLANES, width), F32), slab, slab,
                        pltpu.VMEM((tt, width), F32), pltpu.VMEM((tt, width), F32)],
        compiler_params=_params(2),
        name="lru_mixer",
    )(h, g.reshape(1, d), w_in, conv_w, row(conv_b), wa, row(ba), wx, row(bx), row(lam), w_out)


def _rope(x, cos, sin):
    half = x.shape[1] // 2
    x1, x2 = x[:, :half], x[:, half:]
    return jnp.concatenate([x1 * cos - x2 * sin, x2 * cos + x1 * sin], axis=1)


def _ret_kernel(z_ref, h_ref, cos_ref, sin_ref, wout_ref, o_ref, state, y_scr, *, c, nchunk):
    @pl.when(pl.program_id(1) == 0)
    def _():
        state[...] = jnp.zeros_like(state)

    ri = lax.broadcasted_iota(jnp.int32, (c, c), 0)
    ci = lax.broadcasted_iota(jnp.int32, (c, c), 1)
    diff = (ri - ci).astype(F32)
    pos = lax.broadcasted_iota(jnp.int32, (c, 1), 0).astype(F32)
    qk_w = RET_HEADS * RET_DK
    v_w = RET_HEADS * RET_DV
    for q_ in range(nchunk):
        rows = slice(q_ * c, (q_ + 1) * c)
        cos = cos_ref[rows, :]
        sin = sin_ref[rows, :]
        for hd in range(RET_HEADS):
            log_gamma = math.log1p(-(2.0 ** (-5.0 - hd)))
            q = _rope(z_ref[0, rows, hd * RET_DK:(hd + 1) * RET_DK], cos, sin)
            k = _rope(z_ref[0, rows, qk_w + hd * RET_DK:qk_w + (hd + 1) * RET_DK], cos, sin) * (RET_DK ** -0.5)
            v = z_ref[0, rows, 2 * qk_w + hd * RET_DV:2 * qk_w + (hd + 1) * RET_DV].astype(BF16)
            gt = z_ref[0, rows, 2 * qk_w + v_w + hd * RET_DV:2 * qk_w + v_w + (hd + 1) * RET_DV]
            dmask = jnp.where(diff >= 0.0, jnp.exp(log_gamma * jnp.maximum(diff, 0.0)), 0.0)
            inter_dec = jnp.exp(log_gamma * (pos + 1.0))
            k_dec = jnp.exp(log_gamma * (c - 1.0 - pos))
            chunk_dec = math.exp(log_gamma * c)
            qb = q.astype(BF16)
            scores = _dot_nt(qb, k.astype(BF16)) * dmask
            s_old = state[hd]
            o = _dot(scores.astype(BF16), v) + _dot(qb, s_old.astype(BF16)) * inter_dec
            state[hd] = s_old * chunk_dec + _dot_tn((k * k_dec).astype(BF16), v)
            y = _silu(gt) * _rms(o)
            y_scr[rows, hd * RET_DV:(hd + 1) * RET_DV] = y.astype(BF16)
    o_ref[0] = h_ref[0] + _dot(y_scr[...], wout_ref[...])


def _ret_layer(h, g, w_in, w_out, cos, sin, c=256, nchunk=2):
    b_, s_, d = h.shape
    n_in = w_in.shape[1]
    v_w = w_out.shape[0]
    tt = c * nchunk
    z = _norm_matmul(h.reshape(b_ * s_, d), g, w_in.astype(BF16), 512).reshape(b_, s_, n_in)
    return pl.pallas_call(
        functools.partial(_ret_kernel, c=c, nchunk=nchunk),
        grid=(b_, s_ // tt),
        in_specs=[pl.BlockSpec((1, tt, n_in), lambda bi, ti: (bi, ti, 0)),
                  pl.BlockSpec((1, tt, d), lambda bi, ti: (bi, ti, 0)),
                  pl.BlockSpec((tt, RET_DK // 2), lambda bi, ti: (ti, 0)),
                  pl.BlockSpec((tt, RET_DK // 2), lambda bi, ti: (ti, 0)),
                  pl.BlockSpec((v_w, d), lambda bi, ti: (0, 0), pipeline_mode=pl.Buffered(1))],
        out_specs=pl.BlockSpec((1, tt, d), lambda bi, ti: (bi, ti, 0)),
        out_shape=jax.ShapeDtypeStruct((b_, s_, d), F32),
        scratch_shapes=[pltpu.VMEM((RET_HEADS, RET_DK, RET_DV), F32),
                        pltpu.VMEM((tt, v_w), BF16)],
        compiler_params=_params(2),
        name="retention_mixer",
    )(z, h, cos, sin, w_out.astype(BF16))


GDN_CHUNKS_PER_STEP = 2
GDN_SIDE_UNITS = 3


def _interleave(main, side, per_step):
    for _ in main:
        for _ in range(per_step):
            next(side, None)
    for _ in side:
        pass


def _gdn_kernel(z_ref, h_ref, cw_ref, alog_ref, dtb_ref, nw_ref, wout_ref, o_ref,
                tail, qkv, qn, kn, state, y_scr, *, c, nchunk):
    nq, nv, dh = GDN_QK_HEADS, GDN_V_HEADS, GDN_D
    qk_w = nq * dh
    conv_ch = 2 * qk_w + nv * dh
    rep = nv // nq
    n = c // SUBLANES

    @pl.when(pl.program_id(1) == 0)
    def _():
        tail[...] = jnp.zeros_like(tail)
        state[...] = jnp.zeros_like(state)

    ri = lax.broadcasted_iota(jnp.int32, (c, c), 0)
    ci = lax.broadcasted_iota(jnp.int32, (c, c), 1)
    eye = ri == ci
    strict = ri > ci
    same = lambda sh: (ri >> sh) == (ci >> sh)
    levels = []
    sh = 4
    while (1 << sh) < c:
        levels.append(same(sh + 1) & jnp.logical_not(same(sh)) & strict)
        sh += 1
    diag_blocks = same(4) & strict
    plus_eye = lambda p: jnp.where(eye, 1.0, p).astype(BF16)
    heads = range(nv)

    def prep(q):
        ctx = dict(prods=[], x0b=[], f0b=[], attnb=[], rhsb=[], q_dec=[], k_decb=[], g_tot=[],
                   lowb=[[] for _ in levels])
        for k in range(conv_ch // LANES):
            xs = [z_ref[q, k, _slab_group(j, n), :] for j in range(n)]
            for j, xc in enumerate(_conv_steps(xs, cw_ref, tail, slice(k * LANES, (k + 1) * LANES))):
                qkv[q, k, _slab_group(j, n), :] = _silu(xc)
            yield
        zs = _slab_get(z_ref.at[q, (conv_ch + nv * dh) // LANES], n)
        beta_all = _sigmoid(zs)
        gcum = _cumsum_rows(-jnp.exp(alog_ref[...]) * _softplus(zs + dtb_ref[...]))
        gcum_t = gcum.T
        yield
        for j in range(nq):
            qj = _slab_get(qkv.at[q, j], n)
            kj = _slab_get(qkv.at[q, nq + j], n)
            qn[q, j] = qj * lax.rsqrt(jnp.sum(qj * qj, axis=-1, keepdims=True) + L2_EPS) * (dh ** -0.5)
            kn[q, j] = kj * lax.rsqrt(jnp.sum(kj * kj, axis=-1, keepdims=True) + L2_EPS)
            ctx["prods"].append(_dot_nt(jnp.concatenate([kn[q, j], qn[q, j]], axis=0).astype(BF16),
                                        kn[q, j].astype(BF16)))
            yield
        for i in heads:
            j = i // rep
            beta = beta_all[:, i:i + 1]
            g_col = gcum[:, nv + i:nv + i + 1]
            g_row = gcum_t[nv + i:nv + i + 1, :]
            g_end = g_col[c - 1:c, :]
            decay = jnp.where(ri >= ci, jnp.exp(jnp.minimum(g_col - g_row, 0.0)), 0.0)
            a_mat = beta * ctx["prods"][j][:c] * decay
            x0 = jnp.where(diag_blocks, -a_mat, 0.0)
            ctx["x0b"].append(x0.astype(BF16))
            ctx["f0b"].append(jnp.where(eye, 1.0, x0).astype(BF16))
            for lvl, mask in enumerate(levels):
                ctx["lowb"][lvl].append(jnp.where(mask, a_mat, 0.0).astype(BF16))
            ctx["attnb"].append((ctx["prods"][j][c:] * decay).astype(BF16))
            exp_g = jnp.exp(g_col)
            v = _slab_get(qkv.at[q, 2 * nq + i], n)
            ctx["rhsb"].append(jnp.concatenate([v * beta, kn[q, j] * (beta * exp_g)], axis=1).astype(BF16))
            ctx["q_dec"].append(qn[q, j] * exp_g)
            ctx["k_decb"].append((kn[q, j] * jnp.exp(g_end - g_col)).astype(BF16))
            ctx["g_tot"].append(jnp.exp(g_end))
            yield
        ctxs[q] = ctx

    def mxu(q):
        cx = ctxs[q]
        p1 = [_dot(x, x) for x in cx["x0b"]]
        yield
        m1 = [_dot(f0, plus_eye(p)).astype(BF16) for f0, p in zip(cx["f0b"], p1)]
        p1b = [p.astype(BF16) for p in p1]
        yield
        p2 = [_dot(p, p) for p in p1b]
        yield
        p2b = [p.astype(BF16) for p in p2]
        p3 = [_dot(pb, pb) for pb in p2b]
        yield
        m2 = [_dot(plus_eye(a), plus_eye(b_)).astype(BF16) for a, b_ in zip(p2, p3)]
        yield
        tb = [_dot(u, w).astype(BF16) for u, w in zip(m1, m2)]
        yield
        for lows in cx["lowb"]:
            corr = [jnp.where(eye, 1.0, -_dot(low, t)).astype(BF16) for low, t in zip(lows, tb)]
            yield
            tb = [_dot(t, g).astype(BF16) for t, g in zip(tb, corr)]
            yield
        uw = [_dot(t, r) for t, r in zip(tb, cx["rhsb"])]
        yield
        s_old = [state[i] for i in heads]
        wq = [_dot(jnp.concatenate([uw[i][:, dh:], cx["q_dec"][i]], axis=0).astype(BF16),
                   s_old[i].astype(BF16)) for i in heads]
        yield
        v_newb = [(uw[i][:, :dh] - wq[i][:c]).astype(BF16) for i in heads]
        cx["o"] = [wq[i][c:] + _dot(cx["attnb"][i], v_newb[i]) for i in heads]
        yield
        for i in heads:
            state[i] = s_old[i] * cx["g_tot"][i] + _dot_tn(cx["k_decb"][i], v_newb[i])
        yield

    def post(q):
        nw = nw_ref[...]
        for i in heads:
            zg = _slab_get(z_ref.at[q, conv_ch // LANES + i], n)
            y_scr[q * c:(q + 1) * c, i * dh:(i + 1) * dh] = (_rms(ctxs[q]["o"][i]) * nw * _silu(zg)).astype(BF16)
            yield

    ctxs = {}
    for _ in prep(0):
        pass
    for q in range(nchunk):
        side = itertools.chain(post(q - 1) if q > 0 else (), prep(q + 1) if q + 1 < nchunk else ())
        _interleave(mxu(q), side, GDN_SIDE_UNITS)
    for _ in post(nchunk - 1):
        pass
    o_ref[0] = h_ref[0] + _dot(y_scr[...], wout_ref[...])


def _gdn_layer(h, g, w_in, conv_w, a_log, dt_bias, norm_w, w_out, c=128, nchunk=GDN_CHUNKS_PER_STEP):
    b_, s_, d = h.shape
    v_w = w_out.shape[0]
    conv_ch = conv_w.shape[1]
    main_w = conv_ch + v_w
    small = w_in.shape[1] - main_w
    w_tail = jnp.pad(w_in[:, main_w:], ((0, 0), (0, LANES - small))).astype(BF16)
    n_cat = main_w + LANES
    z = _norm_matmul(h.reshape(b_ * s_, d), g, w_in[:, :main_w].astype(BF16), 512, slab_rows=c,
                     w_tail=w_tail)
    tt = c * nchunk
    nt = s_ // tt
    pad_row = lambda v: jnp.zeros((1, LANES), F32).at[0, GDN_V_HEADS:2 * GDN_V_HEADS].set(v)
    const2 = lambda bi, ti: (0, 0)
    rows = _slab_rows(c // SUBLANES)
    return pl.pallas_call(
        functools.partial(_gdn_kernel, c=c, nchunk=nchunk),
        grid=(b_, nt),
        in_specs=[pl.BlockSpec((nchunk, n_cat // LANES, rows, LANES), lambda bi, ti: (bi * nt + ti, 0, 0, 0)),
                  pl.BlockSpec((1, tt, d), lambda bi, ti: (bi, ti, 0)),
                  pl.BlockSpec((CONV_WIDTH, conv_ch), const2),
                  pl.BlockSpec((1, LANES), const2),
                  pl.BlockSpec((1, LANES), const2),
                  pl.BlockSpec((1, GDN_D), const2),
                  pl.BlockSpec((v_w, d), const2)],
        out_specs=pl.BlockSpec((1, tt, d), lambda bi, ti: (bi, ti, 0)),
        out_shape=jax.ShapeDtypeStruct((b_, s_, d), F32),
        scratch_shapes=[pltpu.VMEM((CONV_WIDTH - 1, SUBLANES, conv_ch), F32),
                        pltpu.VMEM((nchunk, conv_ch // LANES, rows, LANES), F32),
                        pltpu.VMEM((nchunk, GDN_QK_HEADS, c, GDN_D), F32),
                        pltpu.VMEM((nchunk, GDN_QK_HEADS, c, GDN_D), F32),
                        pltpu.VMEM((GDN_V_HEADS, GDN_D, GDN_D), F32),
                        pltpu.VMEM((tt, v_w), BF16)],
        compiler_params=_params(2),
        name="gdn_mixer",
    )(z, h, conv_w, pad_row(a_log), pad_row(dt_bias), norm_w.reshape(1, GDN_D),
      w_out.astype(BF16))


def kernel(x, mix_norm, mlp_norm, mlp_w_up, mlp_w_down, lru_w_in, lru_conv_w, lru_conv_b, lru_wa, lru_ba, lru_wx, lru_bx, lru_lambda, lru_w_out, ret_w_in, ret_w_out, gdn_w_in, gdn_conv_w, gdn_a_log, gdn_dt_bias, gdn_norm, gdn_w_out, final_norm):
    b_, s_, d = x.shape
    depth = mix_norm.shape[0]
    pos = jnp.arange(s_, dtype=F32)
    inv_freq = ROPE_BASE ** (-jnp.arange(0, RET_DK, 2, dtype=F32) / RET_DK)
    ang = pos[:, None] * inv_freq[None, :]
    cos, sin = jnp.cos(ang), jnp.sin(ang)
    lru_w = [w.astype(BF16) for w in (lru_w_in, lru_wa, lru_wx, lru_w_out)]
    h = x
    for i in range(depth):
        kind, j = i % 3, i // 3
        if kind == 0:
            h = _lru_layer(h, mix_norm[i], j, *lru_w, lru_conv_w[j], lru_conv_b[j],
                           lru_ba[j].reshape(-1), lru_bx[j].reshape(-1), lru_lambda[j])
        elif kind == 1:
            h = _ret_layer(h, mix_norm[i], ret_w_in[j], ret_w_out[j], cos, sin)
        else:
            h = _gdn_layer(h, mix_norm[i], gdn_w_in[j], gdn_conv_w[j], gdn_a_log[j],
                           gdn_dt_bias[j], gdn_norm[j], gdn_w_out[j])
        h = _mlp(h.reshape(b_ * s_, d), mlp_norm[i], mlp_w_up, mlp_w_down, i, final_norm, i == depth - 1,
                 1024, 1024).reshape(b_, s_, d)
    return h
```

```python
import functools
import itertools
import math

import jax
import jax.numpy as jnp
from jax import lax
from jax.experimental import pallas as pl
from jax.experimental.pallas import tpu as pltpu

F32 = jnp.float32
BF16 = jnp.bfloat16

RMS_EPS = 1e-6
L2_EPS = 1e-6
CONV_WIDTH = 4
LRU_C = 8.0
LRU_BLOCK = 256
RET_HEADS = 4
RET_DK = 256
RET_DV = 512
ROPE_BASE = 10000.0
GDN_QK_HEADS = 8
GDN_V_HEADS = 16
GDN_D = 128
LANES = 128
SUBLANES = 8
VMEM_LIMIT = 56 * 1024 * 1024

IN_PROJ_ROWS = 512
LRU_ROWS = 512
RET_CHUNK = 256
RET_CHUNKS_PER_STEP = 2
GDN_CHUNK = 128
MLP_ROWS = 1024
MLP_FF_BLOCK = 1024

NT_DIMS = (((1,), (1,)), ((), ()))
TN_DIMS = (((0,), (0,)), ((), ()))


def _dot(a, b):
    return jnp.dot(a, b, preferred_element_type=F32)


def _dot_nt(a, b):
    return lax.dot_general(a, b, NT_DIMS, preferred_element_type=F32)


def _dot_tn(a, b):
    return lax.dot_general(a, b, TN_DIMS, preferred_element_type=F32)


def _rms(x):
    return x * lax.rsqrt(jnp.mean(x * x, axis=-1, keepdims=True) + RMS_EPS)


def _sigmoid(x):
    return 0.5 * jnp.tanh(0.5 * x) + 0.5


def _silu(x):
    half = 0.5 * x
    return half * jnp.tanh(half) + half


def _softplus(x):
    return jnp.maximum(x, 0.0) + jnp.log1p(jnp.exp(-jnp.abs(x)))


def _params(n_axes):
    return pltpu.CompilerParams(dimension_semantics=("arbitrary",) * n_axes,
                                vmem_limit_bytes=VMEM_LIMIT)


def _slab_rows(n):
    return SUBLANES * (n + 1)


def _slab_put(slab, x, n):
    for s in range(SUBLANES):
        slab[pl.ds(s * (n + 1), n), :] = x[s * n:(s + 1) * n, :]


def _slab_get(slab, n):
    return jnp.concatenate([slab[pl.ds(s * (n + 1), n), :] for s in range(SUBLANES)], axis=0)


def _slab_group(j, n):
    return pl.ds(j, SUBLANES, stride=n + 1)


def _norm_matmul_kernel(h_ref, g_ref, w_ref, *rest, slab_rows):
    o_ref = rest[-1]
    u = (_rms(h_ref[...]) * g_ref[...]).astype(BF16)
    z = _dot(u, w_ref[...])
    if slab_rows is None:
        o_ref[...] = z
    else:
        n = slab_rows // SUBLANES
        z2 = _dot(u, rest[0][...])
        nk = z.shape[1] // LANES
        for r in range(z.shape[0] // slab_rows):
            rows = slice(r * slab_rows, (r + 1) * slab_rows)
            for k in range(nk + 1):
                piece = z[rows, k * LANES:(k + 1) * LANES] if k < nk else z2[rows, :]
                _slab_put(o_ref.at[r, k], piece, n)
                o_ref[r, k, _slab_group(n, n), :] = jnp.zeros((SUBLANES, LANES), F32)


def _norm_matmul(h2, g, w, tm, slab_rows=None, w_tail=None):
    t, d = h2.shape
    n = w.shape[1]
    in_specs = [pl.BlockSpec((tm, d), lambda i: (i, 0)),
                pl.BlockSpec((1, d), lambda i: (0, 0)),
                pl.BlockSpec((d, n), lambda i: (0, 0), pipeline_mode=pl.Buffered(1))]
    args = [h2, g.reshape(1, d), w]
    if slab_rows is None:
        out_spec = pl.BlockSpec((tm, n), lambda i: (i, 0))
        out_shape = jax.ShapeDtypeStruct((t, n), F32)
    else:
        rows = _slab_rows(slab_rows // SUBLANES)
        nslab = n // LANES + 1
        in_specs.append(pl.BlockSpec((d, LANES), lambda i: (0, 0), pipeline_mode=pl.Buffered(1)))
        args.append(w_tail)
        out_spec = pl.BlockSpec((tm // slab_rows, nslab, rows, LANES), lambda i: (i, 0, 0, 0))
        out_shape = jax.ShapeDtypeStruct((t // slab_rows, nslab, rows, LANES), F32)
    return pl.pallas_call(
        functools.partial(_norm_matmul_kernel, slab_rows=slab_rows),
        grid=(t // tm,),
        in_specs=in_specs,
        out_specs=out_spec,
        out_shape=out_shape,
        compiler_params=_params(1),
        name="norm_matmul",
    )(*args)


def _mlp_kernel(h_ref, g_ref, wu_ref, wd_ref, fg_ref, o_ref, u_scr, acc_scr, *, final):
    j = pl.program_id(1)

    @pl.when(j == 0)
    def _():
        u_scr[...] = (_rms(h_ref[...]) * g_ref[...]).astype(BF16)
        acc_scr[...] = jnp.zeros_like(acc_scr)

    hid = _dot(u_scr[...], wu_ref[...].astype(BF16))
    hid = jnp.square(jnp.maximum(hid, 0.0)).astype(BF16)
    acc_scr[...] += _dot(hid, wd_ref[...].astype(BF16))

    @pl.when(j == pl.num_programs(1) - 1)
    def _():
        out = h_ref[...] + acc_scr[...]
        if final:
            out = _rms(out) * fg_ref[...]
        o_ref[...] = out


def _mlp(h2, g, wu, wd, layer, fg, final, tm, tf):
    t, d = h2.shape
    ff = wu.shape[2]
    return pl.pallas_call(
        functools.partial(_mlp_kernel, final=final),
        grid=(t // tm, ff // tf),
        in_specs=[pl.BlockSpec((tm, d), lambda i, j: (i, 0)),
                  pl.BlockSpec((1, d), lambda i, j: (0, 0)),
                  pl.BlockSpec((None, d, tf), lambda i, j: (layer, 0, j)),
                  pl.BlockSpec((None, tf, d), lambda i, j: (layer, j, 0)),
                  pl.BlockSpec((1, d), lambda i, j: (0, 0))],
        out_specs=pl.BlockSpec((tm, d), lambda i, j: (i, 0)),
        out_shape=jax.ShapeDtypeStruct((t, d), F32),
        scratch_shapes=[pltpu.VMEM((tm, d), BF16), pltpu.VMEM((tm, d), F32)],
        compiler_params=_params(2),
        name="mlp",
    )(h2, g.reshape(1, d), wu, wd, fg.reshape(1, d))


def _conv_steps(xs, cw_ref, tail, lanes):
    n = len(xs)
    sub = lax.broadcasted_iota(jnp.int32, xs[0].shape, 0)
    hist = [pltpu.roll(jnp.where(sub == SUBLANES - 1, tail[k - 1, :, lanes], xs[n - k]), 1, 0)
            for k in range(1, CONV_WIDTH)]
    taps = [jnp.broadcast_to(cw_ref[k:k + 1, lanes], xs[0].shape) for k in range(CONV_WIDTH)]
    at = lambda j: xs[j] if j >= 0 else hist[-j - 1]
    out = []
    for j in range(n):
        acc = taps[CONV_WIDTH - 1] * xs[j]
        for k in range(1, CONV_WIDTH):
            acc = acc + taps[CONV_WIDTH - 1 - k] * at(j - k)
        out.append(acc)
    for k in range(1, CONV_WIDTH):
        tail[k - 1, :, lanes] = xs[n - k]
    return out


def _cumsum_rows(x):
    n = x.shape[0]
    row = lax.broadcasted_iota(jnp.int32, x.shape, 0)
    d = 1
    while d < n:
        x = x + jnp.where(row >= d, pltpu.roll(x, d, 0), 0.0)
        d *= 2
    return x


def _lru_kernel(h_ref, g_ref, win_ref, cw_ref, cb_ref, wa_ref, ba_ref, wx_ref, bx_ref, lam_ref,
                wout_ref, o_ref, tail, carry, x_slab, hs_slab, h0_scr, pp_scr, *, tt, width):
    n = tt // SUBLANES
    bw = LRU_BLOCK
    nblk = width // bw
    cpb = bw // LANES
    grp = lambda t, j: t[j * SUBLANES:(j + 1) * SUBLANES]
    grp_ds = lambda j: pl.ds(j * SUBLANES, SUBLANES)

    @pl.when(pl.program_id(1) == 0)
    def _():
        tail[...] = jnp.zeros_like(tail)
        carry[...] = jnp.zeros_like(carry)

    hn = h_ref[0]
    u = (_rms(hn) * g_ref[...]).astype(BF16)
    gate, out = {}, [hn]

    def x_proj(blk):
        zx = _dot(u, win_ref[:, width + blk * bw:width + (blk + 1) * bw])
        for c in range(cpb):
            _slab_put(x_slab.at[blk * cpb + c], zx[:, c * LANES:(c + 1) * LANES], n)

    def gate_proj(blk):
        gate[blk] = _dot(u, win_ref[:, blk * bw:(blk + 1) * bw])

    def recur(blk):
        lanes = slice(blk * bw, (blk + 1) * bw)
        xg = [jnp.concatenate(g, axis=1) for g in zip(*[
            _conv_steps([x_slab[blk * cpb + c, _slab_group(j, n), :] for j in range(n)],
                        cw_ref, tail, slice(blk * bw + c * LANES, blk * bw + (c + 1) * LANES))
            for c in range(cpb)])]
        cb = cb_ref[:, lanes]
        xg = [g + cb for g in xg]
        xb = jnp.concatenate(xg, axis=0).astype(BF16)
        ra = _dot(xb, wa_ref[blk])
        ia = _dot(xb, wx_ref[blk])
        ba = ba_ref[:, lanes]
        bx = bx_ref[:, lanes]
        sp = -LRU_C * _softplus(-lam_ref[:, lanes])
        hc = jnp.zeros((SUBLANES, bw), F32)
        pc = jnp.ones((SUBLANES, bw), F32)
        for j in range(n):
            r = _sigmoid(grp(ra, j) + ba)
            i = _sigmoid(grp(ia, j) + bx)
            aj = jnp.exp(r * sp)
            om = 1.0 - aj * aj
            bj = om * lax.rsqrt(jnp.maximum(om, 1e-30)) * (i * xg[j])
            hc = aj * hc + bj
            pc = aj * pc
            h0_scr[grp_ds(j), lanes] = hc
            pp_scr[grp_ds(j), lanes] = pc
        sub = lax.broadcasted_iota(jnp.int32, (SUBLANES, bw), 0)
        ea, eb = pc, hc
        d = 1
        while d < SUBLANES:
            keep = sub >= d
            eb = jnp.where(keep, ea * pltpu.roll(eb, d, 0) + eb, eb)
            ea = jnp.where(keep, ea * pltpu.roll(ea, d, 0), ea)
            d *= 2
        c0 = carry[:, lanes]
        seg_end = eb + ea * c0
        c_in = jnp.where(sub == 0, c0, pltpu.roll(seg_end, 1, 0))
        carry[:, lanes] = jnp.broadcast_to(seg_end[SUBLANES - 1:SUBLANES, :], (SUBLANES, bw))
        for j in range(n):
            hs = h0_scr[grp_ds(j), lanes] + pp_scr[grp_ds(j), lanes] * c_in
            for c in range(cpb):
                hs_slab[blk * cpb + c, _slab_group(j, n), :] = hs[:, c * LANES:(c + 1) * LANES]

    def out_piece(blk):
        hs_time = jnp.concatenate([_slab_get(hs_slab.at[blk * cpb + c], n) for c in range(cpb)], axis=1)
        y = (jax.nn.gelu(gate[blk]) * hs_time).astype(BF16)
        out[0] = out[0] + _dot(y, wout_ref[blk * bw:(blk + 1) * bw, :])

    x_proj(0)
    for blk in range(nblk):
        if blk + 1 < nblk:
            x_proj(blk + 1)
        gate_proj(blk)
        recur(blk)
        if blk > 0:
            out_piece(blk - 1)
    out_piece(nblk - 1)
    o_ref[0] = out[0]


def _lru_layer(h, g, layer, w_in, wa, wx, w_out, conv_w, conv_b, ba, bx, lam, tt=LRU_ROWS):
    b_, s_, d = h.shape
    width = w_out.shape[1]
    row = lambda v: v.reshape(1, width)
    const2 = lambda bi, ti: (0, 0)
    slab = pltpu.VMEM((width // LANES, _slab_rows(tt // SUBLANES), LANES), F32)
    return pl.pallas_call(
        functools.partial(_lru_kernel, tt=tt, width=width),
        grid=(b_, s_ // tt),
        in_specs=[pl.BlockSpec((1, tt, d), lambda bi, ti: (bi, ti, 0)),
                  pl.BlockSpec((1, d), const2),
                  pl.BlockSpec((None, d, 2 * width), lambda bi, ti: (layer, 0, 0)),
                  pl.BlockSpec((CONV_WIDTH, width), const2),
                  pl.BlockSpec((1, width), const2),
                  pl.BlockSpec((None,) + wa.shape[1:], lambda bi, ti: (layer, 0, 0, 0)),
                  pl.BlockSpec((1, width), const2),
                  pl.BlockSpec((None,) + wx.shape[1:], lambda bi, ti: (layer, 0, 0, 0)),
                  pl.BlockSpec((1, width), const2),
                  pl.BlockSpec((1, width), const2),
                  pl.BlockSpec((None, width, d), lambda bi, ti: (layer, 0, 0))],
        out_specs=pl.BlockSpec((1, tt, d), lambda bi, ti: (bi, ti, 0)),
        out_shape=jax.ShapeDtypeStruct((b_, s_, d), F32),
        scratch_shapes=[pltpu.VMEM((CONV_WIDTH - 1, SUBLANES, width), F32),
                        pltpu.VMEM((SUBLANES, width), F32), slab, slab,
                        pltpu.VMEM((tt, width), F32), pltpu.VMEM((tt, width), F32)],
        compiler_params=_params(2),
        name="lru_mixer",
    )(h, g.reshape(1, d), w_in, conv_w, row(conv_b), wa, row(ba), wx, row(bx), row(lam), w_out)


def _rope(x, cos, sin):
    half = x.shape[1] // 2
    x1, x2 = x[:, :half], x[:, half:]
    return jnp.concatenate([x1 * cos - x2 * sin, x2 * cos + x1 * sin], axis=1)


def _ret_kernel(z_ref, h_ref, cos_ref, sin_ref, wout_ref, o_ref, state, y_scr, dmask_scr, *, c, nchunk):
    log_gammas = [math.log1p(-(2.0 ** (-5.0 - hd))) for hd in range(RET_HEADS)]

    @pl.when(pl.program_id(1) == 0)
    def _():
        state[...] = jnp.zeros_like(state)
        ri = lax.broadcasted_iota(jnp.int32, (c, c), 0)
        ci = lax.broadcasted_iota(jnp.int32, (c, c), 1)
        diff = (ri - ci).astype(F32)
        for hd in range(RET_HEADS):
            dmask_scr[hd] = jnp.where(diff >= 0.0, jnp.exp(log_gammas[hd] * jnp.maximum(diff, 0.0)), 0.0)

    pos = lax.broadcasted_iota(jnp.int32, (c, 1), 0).astype(F32)
    qk_w = RET_HEADS * RET_DK
    v_w = RET_HEADS * RET_DV
    for q_ in range(nchunk):
        rows = slice(q_ * c, (q_ + 1) * c)
        cos = cos_ref[rows, :]
        sin = sin_ref[rows, :]
        for hd in range(RET_HEADS):
            log_gamma = log_gammas[hd]
            q = _rope(z_ref[0, rows, hd * RET_DK:(hd + 1) * RET_DK], cos, sin)
            k = _rope(z_ref[0, rows, qk_w + hd * RET_DK:qk_w + (hd + 1) * RET_DK], cos, sin) * (RET_DK ** -0.5)
            v = z_ref[0, rows, 2 * qk_w + hd * RET_DV:2 * qk_w + (hd + 1) * RET_DV].astype(BF16)
            gt = z_ref[0, rows, 2 * qk_w + v_w + hd * RET_DV:2 * qk_w + v_w + (hd + 1) * RET_DV]
            inter_dec = jnp.exp(log_gamma * (pos + 1.0))
            k_dec = jnp.exp(log_gamma * (c - 1.0 - pos))
            chunk_dec = math.exp(log_gamma * c)
            qb = q.astype(BF16)
            scores = _dot_nt(qb, k.astype(BF16)) * dmask_scr[hd]
            s_old = state[hd]
            o = _dot(scores.astype(BF16), v) + _dot(qb, s_old.astype(BF16)) * inter_dec
            state[hd] = s_old * chunk_dec + _dot_tn((k * k_dec).astype(BF16), v)
            y = _silu(gt) * _rms(o)
            y_scr[rows, hd * RET_DV:(hd + 1) * RET_DV] = y.astype(BF16)
    o_ref[0] = h_ref[0] + _dot(y_scr[...], wout_ref[...])


def _ret_layer(h, g, w_in, w_out, cos, sin, c=RET_CHUNK, nchunk=RET_CHUNKS_PER_STEP):
    b_, s_, d = h.shape
    n_in = w_in.shape[1]
    v_w = w_out.shape[0]
    tt = c * nchunk
    z = _norm_matmul(h.reshape(b_ * s_, d), g, w_in.astype(BF16), IN_PROJ_ROWS).reshape(b_, s_, n_in)
    return pl.pallas_call(
        functools.partial(_ret_kernel, c=c, nchunk=nchunk),
        grid=(b_, s_ // tt),
        in_specs=[pl.BlockSpec((1, tt, n_in), lambda bi, ti: (bi, ti, 0)),
                  pl.BlockSpec((1, tt, d), lambda bi, ti: (bi, ti, 0)),
                  pl.BlockSpec((tt, RET_DK // 2), lambda bi, ti: (ti, 0)),
                  pl.BlockSpec((tt, RET_DK // 2), lambda bi, ti: (ti, 0)),
                  pl.BlockSpec((v_w, d), lambda bi, ti: (0, 0), pipeline_mode=pl.Buffered(1))],
        out_specs=pl.BlockSpec((1, tt, d), lambda bi, ti: (bi, ti, 0)),
        out_shape=jax.ShapeDtypeStruct((b_, s_, d), F32),
        scratch_shapes=[pltpu.VMEM((RET_HEADS, RET_DK, RET_DV), F32),
                        pltpu.VMEM((tt, v_w), BF16),
                        pltpu.VMEM((RET_HEADS, c, c), F32)],
        compiler_params=_params(2),
        name="retention_mixer",
    )(z, h, cos, sin, w_out.astype(BF16))


GDN_CHUNKS_PER_STEP = 2
GDN_SIDE_UNITS = 3


def _interleave(main, side, per_step):
    for _ in main:
        for _ in range(per_step):
            next(side, None)
    for _ in side:
        pass


def _gdn_kernel(z_ref, h_ref, cw_ref, alog_ref, dtb_ref, nw_ref, wout_ref, o_ref,
                tail, qkv, qn, kn, state, y_scr, *, c, nchunk):
    nq, nv, dh = GDN_QK_HEADS, GDN_V_HEADS, GDN_D
    qk_w = nq * dh
    conv_ch = 2 * qk_w + nv * dh
    rep = nv // nq
    n = c // SUBLANES

    @pl.when(pl.program_id(1) == 0)
    def _():
        tail[...] = jnp.zeros_like(tail)
        state[...] = jnp.zeros_like(state)

    ri = lax.broadcasted_iota(jnp.int32, (c, c), 0)
    ci = lax.broadcasted_iota(jnp.int32, (c, c), 1)
    eye = ri == ci
    strict = ri > ci
    same = lambda sh: (ri >> sh) == (ci >> sh)
    levels = []
    sh = 4
    while (1 << sh) < c:
        levels.append(same(sh + 1) & jnp.logical_not(same(sh)) & strict)
        sh += 1
    diag_blocks = same(4) & strict
    plus_eye = lambda p: jnp.where(eye, 1.0, p).astype(BF16)
    heads = range(nv)

    def prep(q):
        ctx = dict(prods=[], x0b=[], f0b=[], attnb=[], rhsb=[], q_dec=[], k_decb=[], g_tot=[],
                   lowb=[[] for _ in levels])
        for k in range(conv_ch // LANES):
            xs = [z_ref[q, k, _slab_group(j, n), :] for j in range(n)]
            for j, xc in enumerate(_conv_steps(xs, cw_ref, tail, slice(k * LANES, (k + 1) * LANES))):
                qkv[q, k, _slab_group(j, n), :] = _silu(xc)
            yield
        zs = _slab_get(z_ref.at[q, (conv_ch + nv * dh) // LANES], n)
        beta_all = _sigmoid(zs)
        gcum = _cumsum_rows(-jnp.exp(alog_ref[...]) * _softplus(zs + dtb_ref[...]))
        gcum_t = gcum.T
        yield
        for j in range(nq):
            qj = _slab_get(qkv.at[q, j], n)
            kj = _slab_get(qkv.at[q, nq + j], n)
            qn[q, j] = qj * lax.rsqrt(jnp.sum(qj * qj, axis=-1, keepdims=True) + L2_EPS) * (dh ** -0.5)
            kn[q, j] = kj * lax.rsqrt(jnp.sum(kj * kj, axis=-1, keepdims=True) + L2_EPS)
            ctx["prods"].append(_dot_nt(jnp.concatenate([kn[q, j], qn[q, j]], axis=0).astype(BF16),
                                        kn[q, j].astype(BF16)))
            yield
        for i in heads:
            j = i // rep
            beta = beta_all[:, i:i + 1]
            g_col = gcum[:, nv + i:nv + i + 1]
            g_row = gcum_t[nv + i:nv + i + 1, :]
            g_end = g_col[c - 1:c, :]
            decay = jnp.where(ri >= ci, jnp.exp(jnp.minimum(g_col - g_row, 0.0)), 0.0)
            a_mat = beta * ctx["prods"][j][:c] * decay
            x0 = jnp.where(diag_blocks, -a_mat, 0.0)
            ctx["x0b"].append(x0.astype(BF16))
            ctx["f0b"].append(jnp.where(eye, 1.0, x0).astype(BF16))
            for lvl, mask in enumerate(levels):
                ctx["lowb"][lvl].append(jnp.where(mask, a_mat, 0.0).astype(BF16))
            ctx["attnb"].append((ctx["prods"][j][c:] * decay).astype(BF16))
            exp_g = jnp.exp(g_col)
            v = _slab_get(qkv.at[q, 2 * nq + i], n)
            ctx["rhsb"].append(jnp.concatenate([v * beta, kn[q, j] * (beta * exp_g)], axis=1).astype(BF16))
            ctx["q_dec"].append(qn[q, j] * exp_g)
            ctx["k_decb"].append((kn[q, j] * jnp.exp(g_end - g_col)).astype(BF16))
            ctx["g_tot"].append(jnp.exp(g_end))
            yield
        ctxs[q] = ctx

    def mxu(q):
        cx = ctxs[q]
        p1 = [_dot(x, x) for x in cx["x0b"]]
        yield
        m1 = [_dot(f0, plus_eye(p)).astype(BF16) for f0, p in zip(cx["f0b"], p1)]
        p1b = [p.astype(BF16) for p in p1]
        yield
        p2 = [_dot(p, p) for p in p1b]
        yield
        p2b = [p.astype(BF16) for p in p2]
        p3 = [_dot(pb, pb) for pb in p2b]
        yield
        m2 = [_dot(plus_eye(a), plus_eye(b_)).astype(BF16) for a, b_ in zip(p2, p3)]
        yield
        tb = [_dot(u, w).astype(BF16) for u, w in zip(m1, m2)]
        yield
        for lows in cx["lowb"]:
            corr = [jnp.where(eye, 1.0, -_dot(low, t)).astype(BF16) for low, t in zip(lows, tb)]
            yield
            tb = [_dot(t, g).astype(BF16) for t, g in zip(tb, corr)]
            yield
        uw = [_dot(t, r) for t, r in zip(tb, cx["rhsb"])]
        yield
        s_old = [state[i] for i in heads]
        wq = [_dot(jnp.concatenate([uw[i][:, dh:], cx["q_dec"][i]], axis=0).astype(BF16),
                   s_old[i].astype(BF16)) for i in heads]
        yield
        v_newb = [(uw[i][:, :dh] - wq[i][:c]).astype(BF16) for i in heads]
        cx["o"] = [wq[i][c:] + _dot(cx["attnb"][i], v_newb[i]) for i in heads]
        yield
        for i in heads:
            state[i] = s_old[i] * cx["g_tot"][i] + _dot_tn(cx["k_decb"][i], v_newb[i])
        yield

    def post(q):
        nw = nw_ref[...]
        for i in heads:
            zg = _slab_get(z_ref.at[q, conv_ch // LANES + i], n)
            y_scr[q * c:(q + 1) * c, i * dh:(i + 1) * dh] = (_rms(ctxs[q]["o"][i]) * nw * _silu(zg)).astype(BF16)
            yield

    ctxs = {}
    for _ in prep(0):
        pass
    for q in range(nchunk):
        side = itertools.chain(post(q - 1) if q > 0 else (), prep(q + 1) if q + 1 < nchunk else ())
        _interleave(mxu(q), side, GDN_SIDE_UNITS)
    for _ in post(nchunk - 1):
        pass
    o_ref[0] = h_ref[0] + _dot(y_scr[...], wout_ref[...])


def _gdn_layer(h, g, w_in, conv_w, a_log, dt_bias, norm_w, w_out, c=GDN_CHUNK, nchunk=GDN_CHUNKS_PER_STEP):
    b_, s_, d = h.shape
    v_w = w_out.shape[0]
    conv_ch = conv_w.shape[1]
    main_w = conv_ch + v_w
    small = w_in.shape[1] - main_w
    w_tail = jnp.pad(w_in[:, main_w:], ((0, 0), (0, LANES - small))).astype(BF16)
    n_cat = main_w + LANES
    z = _norm_matmul(h.reshape(b_ * s_, d), g, w_in[:, :main_w].astype(BF16), IN_PROJ_ROWS, slab_rows=c,
                     w_tail=w_tail)
    tt = c * nchunk
    nt = s_ // tt
    pad_row = lambda v: jnp.zeros((1, LANES), F32).at[0, GDN_V_HEADS:2 * GDN_V_HEADS].set(v)
    const2 = lambda bi, ti: (0, 0)
    rows = _slab_rows(c // SUBLANES)
    return pl.pallas_call(
        functools.partial(_gdn_kernel, c=c, nchunk=nchunk),
        grid=(b_, nt),
        in_specs=[pl.BlockSpec((nchunk, n_cat // LANES, rows, LANES), lambda bi, ti: (bi * nt + ti, 0, 0, 0)),
                  pl.BlockSpec((1, tt, d), lambda bi, ti: (bi, ti, 0)),
                  pl.BlockSpec((CONV_WIDTH, conv_ch), const2),
                  pl.BlockSpec((1, LANES), const2),
                  pl.BlockSpec((1, LANES), const2),
                  pl.BlockSpec((1, GDN_D), const2),
                  pl.BlockSpec((v_w, d), const2)],
        out_specs=pl.BlockSpec((1, tt, d), lambda bi, ti: (bi, ti, 0)),
        out_shape=jax.ShapeDtypeStruct((b_, s_, d), F32),
        scratch_shapes=[pltpu.VMEM((CONV_WIDTH - 1, SUBLANES, conv_ch), F32),
                        pltpu.VMEM((nchunk, conv_ch // LANES, rows, LANES), F32),
                        pltpu.VMEM((nchunk, GDN_QK_HEADS, c, GDN_D), F32),
                        pltpu.VMEM((nchunk, GDN_QK_HEADS, c, GDN_D), F32),
                        pltpu.VMEM((GDN_V_HEADS, GDN_D, GDN_D), F32),
                        pltpu.VMEM((tt, v_w), BF16)],
        compiler_params=_params(2),
        name="gdn_mixer",
    )(z, h, conv_w, pad_row(a_log), pad_row(dt_bias), norm_w.reshape(1, GDN_D),
      w_out.astype(BF16))


def kernel(x, mix_norm, mlp_norm, mlp_w_up, mlp_w_down, lru_w_in, lru_conv_w, lru_conv_b, lru_wa, lru_ba, lru_wx, lru_bx, lru_lambda, lru_w_out, ret_w_in, ret_w_out, gdn_w_in, gdn_conv_w, gdn_a_log, gdn_dt_bias, gdn_norm, gdn_w_out, final_norm):
    b_, s_, d = x.shape
    depth = mix_norm.shape[0]
    pos = jnp.arange(s_, dtype=F32)
    inv_freq = ROPE_BASE ** (-jnp.arange(0, RET_DK, 2, dtype=F32) / RET_DK)
    ang = pos[:, None] * inv_freq[None, :]
    cos, sin = jnp.cos(ang), jnp.sin(ang)
    lru_w = [w.astype(BF16) for w in (lru_w_in, lru_wa, lru_wx, lru_w_out)]
    h = x
    for i in range(depth):
        kind, j = i % 3, i // 3
        if kind == 0:
            h = _lru_layer(h, mix_norm[i], j, *lru_w, lru_conv_w[j], lru_conv_b[j],
                           lru_ba[j].reshape(-1), lru_bx[j].reshape(-1), lru_lambda[j])
        elif kind == 1:
            h = _ret_layer(h, mix_norm[i], ret_w_in[j], ret_w_out[j], cos, sin)
        else:
            h = _gdn_layer(h, mix_norm[i], gdn_w_in[j], gdn_conv_w[j], gdn_a_log[j],
                           gdn_dt_bias[j], gdn_norm[j], gdn_w_out[j])
        h = _mlp(h.reshape(b_ * s_, d), mlp_norm[i], mlp_w_up, mlp_w_down, i, final_norm, i == depth - 1,
                 MLP_ROWS, MLP_FF_BLOCK).reshape(b_, s_, d)
    return h
```

```python
import functools
import itertools
import math

import jax
import jax.numpy as jnp
from jax import lax
from jax.experimental import pallas as pl
from jax.experimental.pallas import tpu as pltpu

F32 = jnp.float32
BF16 = jnp.bfloat16

RMS_EPS = 1e-6
L2_EPS = 1e-6
CONV_WIDTH = 4
LRU_C = 8.0
LRU_BLOCK = 256
RET_HEADS = 4
RET_DK = 256
RET_DV = 512
ROPE_BASE = 10000.0
GDN_QK_HEADS = 8
GDN_V_HEADS = 16
GDN_D = 128
LANES = 128
SUBLANES = 8
VMEM_LIMIT = 56 * 1024 * 1024

IN_PROJ_ROWS = 512
LRU_ROWS = 512
RET_CHUNK = 256
RET_CHUNKS_PER_STEP = 2
GDN_CHUNK = 128
MLP_ROWS = 1024
MLP_FF_BLOCK = 1024

NT_DIMS = (((1,), (1,)), ((), ()))
TN_DIMS = (((0,), (0,)), ((), ()))


def _dot(a, b):
    return jnp.dot(a, b, preferred_element_type=F32)


def _dot_nt(a, b):
    return lax.dot_general(a, b, NT_DIMS, preferred_element_type=F32)


def _dot_tn(a, b):
    return lax.dot_general(a, b, TN_DIMS, preferred_element_type=F32)


def _rms(x):
    return x * lax.rsqrt(jnp.mean(x * x, axis=-1, keepdims=True) + RMS_EPS)


def _sigmoid(x):
    return 0.5 * jnp.tanh(0.5 * x) + 0.5


def _silu(x):
    half = 0.5 * x
    return half * jnp.tanh(half) + half


def _softplus(x):
    return jnp.maximum(x, 0.0) + jnp.log1p(jnp.exp(-jnp.abs(x)))


def _params(n_axes):
    return pltpu.CompilerParams(dimension_semantics=("arbitrary",) * n_axes,
                                vmem_limit_bytes=VMEM_LIMIT)


def _slab_rows(n):
    return SUBLANES * (n + 1)


def _slab_put(slab, x, n):
    for s in range(SUBLANES):
        slab[pl.ds(s * (n + 1), n), :] = x[s * n:(s + 1) * n, :]


def _slab_get(slab, n):
    return jnp.concatenate([slab[pl.ds(s * (n + 1), n), :] for s in range(SUBLANES)], axis=0)


def _slab_group(j, n):
    return pl.ds(j, SUBLANES, stride=n + 1)


def _norm_matmul_kernel(h_ref, g_ref, w_ref, *rest, slab_rows):
    o_ref = rest[-1]
    u = (_rms(h_ref[...]) * g_ref[...]).astype(BF16)
    z = _dot(u, w_ref[...])
    if slab_rows is None:
        o_ref[...] = z
    else:
        n = slab_rows // SUBLANES
        z2 = _dot(u, rest[0][...])
        nk = z.shape[1] // LANES
        for r in range(z.shape[0] // slab_rows):
            rows = slice(r * slab_rows, (r + 1) * slab_rows)
            for k in range(nk + 1):
                piece = z[rows, k * LANES:(k + 1) * LANES] if k < nk else z2[rows, :]
                _slab_put(o_ref.at[r, k], piece, n)
                o_ref[r, k, _slab_group(n, n), :] = jnp.zeros((SUBLANES, LANES), F32)


def _norm_matmul(h2, g, w, tm, slab_rows=None, w_tail=None):
    t, d = h2.shape
    n = w.shape[1]
    in_specs = [pl.BlockSpec((tm, d), lambda i: (i, 0)),
                pl.BlockSpec((1, d), lambda i: (0, 0)),
                pl.BlockSpec((d, n), lambda i: (0, 0), pipeline_mode=pl.Buffered(1))]
    args = [h2, g.reshape(1, d), w]
    if slab_rows is None:
        out_spec = pl.BlockSpec((tm, n), lambda i: (i, 0))
        out_shape = jax.ShapeDtypeStruct((t, n), F32)
    else:
        rows = _slab_rows(slab_rows // SUBLANES)
        nslab = n // LANES + 1
        in_specs.append(pl.BlockSpec((d, LANES), lambda i: (0, 0), pipeline_mode=pl.Buffered(1)))
        args.append(w_tail)
        out_spec = pl.BlockSpec((tm // slab_rows, nslab, rows, LANES), lambda i: (i, 0, 0, 0))
        out_shape = jax.ShapeDtypeStruct((t // slab_rows, nslab, rows, LANES), F32)
    return pl.pallas_call(
        functools.partial(_norm_matmul_kernel, slab_rows=slab_rows),
        grid=(t // tm,),
        in_specs=in_specs,
        out_specs=out_spec,
        out_shape=out_shape,
        compiler_params=_params(1),
        name="norm_matmul",
    )(*args)


def _mlp_kernel(h_ref, g_ref, wu_ref, wd_ref, fg_ref, o_ref, u_scr, *, final):
    j = pl.program_id(1)

    @pl.when(j == 0)
    def _():
        h = h_ref[...]
        u_scr[...] = (_rms(h) * g_ref[...]).astype(BF16)
        o_ref[...] = h

    hid = _dot(u_scr[...], wu_ref[...].astype(BF16))
    hid = jnp.square(jnp.maximum(hid, 0.0)).astype(BF16)
    o_ref[...] += _dot(hid, wd_ref[...].astype(BF16))

    if final:
        @pl.when(j == pl.num_programs(1) - 1)
        def _():
            o_ref[...] = _rms(o_ref[...]) * fg_ref[...]


def _mlp(h2, g, wu, wd, layer, fg, final, tm, tf):
    t, d = h2.shape
    ff = wu.shape[2]
    return pl.pallas_call(
        functools.partial(_mlp_kernel, final=final),
        grid=(t // tm, ff // tf),
        in_specs=[pl.BlockSpec((tm, d), lambda i, j: (i, 0)),
                  pl.BlockSpec((1, d), lambda i, j: (0, 0)),
                  pl.BlockSpec((None, d, tf), lambda i, j: (layer, 0, j)),
                  pl.BlockSpec((None, tf, d), lambda i, j: (layer, j, 0)),
                  pl.BlockSpec((1, d), lambda i, j: (0, 0))],
        out_specs=pl.BlockSpec((tm, d), lambda i, j: (i, 0)),
        out_shape=jax.ShapeDtypeStruct((t, d), F32),
        scratch_shapes=[pltpu.VMEM((tm, d), BF16)],
        compiler_params=_params(2),
        name="mlp",
    )(h2, g.reshape(1, d), wu, wd, fg.reshape(1, d))


def _conv_steps(xs, cw_ref, tail, lanes):
    n = len(xs)
    sub = lax.broadcasted_iota(jnp.int32, xs[0].shape, 0)
    hist = [pltpu.roll(jnp.where(sub == SUBLANES - 1, tail[k - 1, :, lanes], xs[n - k]), 1, 0)
            for k in range(1, CONV_WIDTH)]
    taps = [jnp.broadcast_to(cw_ref[k:k + 1, lanes], xs[0].shape) for k in range(CONV_WIDTH)]
    at = lambda j: xs[j] if j >= 0 else hist[-j - 1]
    out = []
    for j in range(n):
        acc = taps[CONV_WIDTH - 1] * xs[j]
        for k in range(1, CONV_WIDTH):
            acc = acc + taps[CONV_WIDTH - 1 - k] * at(j - k)
        out.append(acc)
    for k in range(1, CONV_WIDTH):
        tail[k - 1, :, lanes] = xs[n - k]
    return out


def _cumsum_rows(x):
    n = x.shape[0]
    row = lax.broadcasted_iota(jnp.int32, x.shape, 0)
    d = 1
    while d < n:
        x = x + jnp.where(row >= d, pltpu.roll(x, d, 0), 0.0)
        d *= 2
    return x


def _lru_kernel(h_ref, g_ref, win_ref, cw_ref, cb_ref, wa_ref, ba_ref, wx_ref, bx_ref, lam_ref,
                wout_ref, o_ref, tail, carry, x_slab, hs_slab, h0_scr, pp_scr, *, tt, width):
    n = tt // SUBLANES
    bw = LRU_BLOCK
    nblk = width // bw
    cpb = bw // LANES
    grp = lambda t, j: t[j * SUBLANES:(j + 1) * SUBLANES]
    grp_ds = lambda j: pl.ds(j * SUBLANES, SUBLANES)

    @pl.when(pl.program_id(1) == 0)
    def _():
        tail[...] = jnp.zeros_like(tail)
        carry[...] = jnp.zeros_like(carry)

    hn = h_ref[0]
    u = (_rms(hn) * g_ref[...]).astype(BF16)
    gate, out = {}, [hn]

    def x_proj(blk):
        zx = _dot(u, win_ref[:, width + blk * bw:width + (blk + 1) * bw])
        for c in range(cpb):
            _slab_put(x_slab.at[blk * cpb + c], zx[:, c * LANES:(c + 1) * LANES], n)

    def gate_proj(blk):
        gate[blk] = _dot(u, win_ref[:, blk * bw:(blk + 1) * bw])

    def recur(blk):
        lanes = slice(blk * bw, (blk + 1) * bw)
        xg = [jnp.concatenate(g, axis=1) for g in zip(*[
            _conv_steps([x_slab[blk * cpb + c, _slab_group(j, n), :] for j in range(n)],
                        cw_ref, tail, slice(blk * bw + c * LANES, blk * bw + (c + 1) * LANES))
            for c in range(cpb)])]
        cb = cb_ref[:, lanes]
        xg = [g + cb for g in xg]
        xb = jnp.concatenate(xg, axis=0).astype(BF16)
        ra = _dot(xb, wa_ref[blk])
        ia = _dot(xb, wx_ref[blk])
        ba = ba_ref[:, lanes]
        bx = bx_ref[:, lanes]
        sp = -LRU_C * _softplus(-lam_ref[:, lanes])
        hc = jnp.zeros((SUBLANES, bw), F32)
        pc = jnp.ones((SUBLANES, bw), F32)
        for j in range(n):
            r = _sigmoid(grp(ra, j) + ba)
            i = _sigmoid(grp(ia, j) + bx)
            aj = jnp.exp(r * sp)
            om = 1.0 - aj * aj
            bj = om * lax.rsqrt(jnp.maximum(om, 1e-30)) * (i * xg[j])
            hc = aj * hc + bj
            pc = aj * pc
            h0_scr[grp_ds(j), lanes] = hc
            pp_scr[grp_ds(j), lanes] = pc
        sub = lax.broadcasted_iota(jnp.int32, (SUBLANES, bw), 0)
        ea, eb = pc, hc
        d = 1
        while d < SUBLANES:
            keep = sub >= d
            eb = jnp.where(keep, ea * pltpu.roll(eb, d, 0) + eb, eb)
            ea = jnp.where(keep, ea * pltpu.roll(ea, d, 0), ea)
            d *= 2
        c0 = carry[:, lanes]
        seg_end = eb + ea * c0
        c_in = jnp.where(sub == 0, c0, pltpu.roll(seg_end, 1, 0))
        carry[:, lanes] = jnp.broadcast_to(seg_end[SUBLANES - 1:SUBLANES, :], (SUBLANES, bw))
        for j in range(n):
            hs = h0_scr[grp_ds(j), lanes] + pp_scr[grp_ds(j), lanes] * c_in
            for c in range(cpb):
                hs_slab[blk * cpb + c, _slab_group(j, n), :] = hs[:, c * LANES:(c + 1) * LANES]

    def out_piece(blk):
        hs_time = jnp.concatenate([_slab_get(hs_slab.at[blk * cpb + c], n) for c in range(cpb)], axis=1)
        y = (jax.nn.gelu(gate[blk]) * hs_time).astype(BF16)
        out[0] = out[0] + _dot(y, wout_ref[blk * bw:(blk + 1) * bw, :])

    x_proj(0)
    for blk in range(nblk):
        if blk + 1 < nblk:
            x_proj(blk + 1)
        gate_proj(blk)
        recur(blk)
        if blk > 0:
            out_piece(blk - 1)
    out_piece(nblk - 1)
    o_ref[0] = out[0]


def _lru_layer(h, g, layer, w_in, wa, wx, w_out, conv_w, conv_b, ba, bx, lam, tt=LRU_ROWS):
    b_, s_, d = h.shape
    width = w_out.shape[1]
    row = lambda v: v.reshape(1, width)
    const2 = lambda bi, ti: (0, 0)
    slab = pltpu.VMEM((width // LANES, _slab_rows(tt // SUBLANES), LANES), F32)
    return pl.pallas_call(
        functools.partial(_lru_kernel, tt=tt, width=width),
        grid=(b_, s_ // tt),
        in_specs=[pl.BlockSpec((1, tt, d), lambda bi, ti: (bi, ti, 0)),
                  pl.BlockSpec((1, d), const2),
                  pl.BlockSpec((None, d, 2 * width), lambda bi, ti: (layer, 0, 0)),
                  pl.BlockSpec((CONV_WIDTH, width), const2),
                  pl.BlockSpec((1, width), const2),
                  pl.BlockSpec((None,) + wa.shape[1:], lambda bi, ti: (layer, 0, 0, 0)),
                  pl.BlockSpec((1, width), const2),
                  pl.BlockSpec((None,) + wx.shape[1:], lambda bi, ti: (layer, 0, 0, 0)),
                  pl.BlockSpec((1, width), const2),
                  pl.BlockSpec((1, width), const2),
                  pl.BlockSpec((None, width, d), lambda bi, ti: (layer, 0, 0))],
        out_specs=pl.BlockSpec((1, tt, d), lambda bi, ti: (bi, ti, 0)),
        out_shape=jax.ShapeDtypeStruct((b_, s_, d), F32),
        scratch_shapes=[pltpu.VMEM((CONV_WIDTH - 1, SUBLANES, width), F32),
                        pltpu.VMEM((SUBLANES, width), F32), slab, slab,
                        pltpu.VMEM((tt, width), F32), pltpu.VMEM((tt, width), F32)],
        compiler_params=_params(2),
        name="lru_mixer",
    )(h, g.reshape(1, d), w_in, conv_w, row(conv_b), wa, row(ba), wx, row(bx), row(lam), w_out)


def _rope(x, cos, sin):
    half = x.shape[1] // 2
    x1, x2 = x[:, :half], x[:, half:]
    return jnp.concatenate([x1 * cos - x2 * sin, x2 * cos + x1 * sin], axis=1)


def _ret_kernel(z_ref, h_ref, cos_ref, sin_ref, wout_ref, o_ref, state, y_scr, dmask_scr, *, c, nchunk):
    log_gammas = [math.log1p(-(2.0 ** (-5.0 - hd))) for hd in range(RET_HEADS)]

    @pl.when(pl.program_id(1) == 0)
    def _():
        state[...] = jnp.zeros_like(state)
        ri = lax.broadcasted_iota(jnp.int32, (c, c), 0)
        ci = lax.broadcasted_iota(jnp.int32, (c, c), 1)
        diff = (ri - ci).astype(F32)
        for hd in range(RET_HEADS):
            dmask_scr[hd] = jnp.where(diff >= 0.0, jnp.exp(log_gammas[hd] * jnp.maximum(diff, 0.0)), 0.0)

    pos = lax.broadcasted_iota(jnp.int32, (c, 1), 0).astype(F32)
    qk_w = RET_HEADS * RET_DK
    v_w = RET_HEADS * RET_DV
    for q_ in range(nchunk):
        rows = slice(q_ * c, (q_ + 1) * c)
        cos = cos_ref[rows, :]
        sin = sin_ref[rows, :]
        for hd in range(RET_HEADS):
            log_gamma = log_gammas[hd]
            q = _rope(z_ref[0, rows, hd * RET_DK:(hd + 1) * RET_DK], cos, sin)
            k = _rope(z_ref[0, rows, qk_w + hd * RET_DK:qk_w + (hd + 1) * RET_DK], cos, sin) * (RET_DK ** -0.5)
            v = z_ref[0, rows, 2 * qk_w + hd * RET_DV:2 * qk_w + (hd + 1) * RET_DV].astype(BF16)
            gt = z_ref[0, rows, 2 * qk_w + v_w + hd * RET_DV:2 * qk_w + v_w + (hd + 1) * RET_DV]
            inter_dec = jnp.exp(log_gamma * (pos + 1.0))
            k_dec = jnp.exp(log_gamma * (c - 1.0 - pos))
            chunk_dec = math.exp(log_gamma * c)
            qb = q.astype(BF16)
            scores = _dot_nt(qb, k.astype(BF16)) * dmask_scr[hd]
            s_old = state[hd]
            o = _dot(scores.astype(BF16), v) + _dot(qb, s_old.astype(BF16)) * inter_dec
            state[hd] = s_old * chunk_dec + _dot_tn((k * k_dec).astype(BF16), v)
            y = _silu(gt) * _rms(o)
            y_scr[rows, hd * RET_DV:(hd + 1) * RET_DV] = y.astype(BF16)
    o_ref[0] = h_ref[0] + _dot(y_scr[...], wout_ref[...])


def _ret_layer(h, g, w_in, w_out, cos, sin, c=RET_CHUNK, nchunk=RET_CHUNKS_PER_STEP):
    b_, s_, d = h.shape
    n_in = w_in.shape[1]
    v_w = w_out.shape[0]
    tt = c * nchunk
    z = _norm_matmul(h.reshape(b_ * s_, d), g, w_in.astype(BF16), IN_PROJ_ROWS).reshape(b_, s_, n_in)
    return pl.pallas_call(
        functools.partial(_ret_kernel, c=c, nchunk=nchunk),
        grid=(b_, s_ // tt),
        in_specs=[pl.BlockSpec((1, tt, n_in), lambda bi, ti: (bi, ti, 0)),
                  pl.BlockSpec((1, tt, d), lambda bi, ti: (bi, ti, 0)),
                  pl.BlockSpec((tt, RET_DK // 2), lambda bi, ti: (ti, 0)),
                  pl.BlockSpec((tt, RET_DK // 2), lambda bi, ti: (ti, 0)),
                  pl.BlockSpec((v_w, d), lambda bi, ti: (0, 0), pipeline_mode=pl.Buffered(1))],
        out_specs=pl.BlockSpec((1, tt, d), lambda bi, ti: (bi, ti, 0)),
        out_shape=jax.ShapeDtypeStruct((b_, s_, d), F32),
        scratch_shapes=[pltpu.VMEM((RET_HEADS, RET_DK, RET_DV), F32),
                        pltpu.VMEM((tt, v_w), BF16),
                        pltpu.VMEM((RET_HEADS, c, c), F32)],
        compiler_params=_params(2),
        name="retention_mixer",
    )(z, h, cos, sin, w_out.astype(BF16))


GDN_CHUNKS_PER_STEP = 2
GDN_SIDE_UNITS = 3


def _interleave(main, side, per_step):
    for _ in main:
        for _ in range(per_step):
            next(side, None)
    for _ in side:
        pass


def _gdn_kernel(z_ref, h_ref, cw_ref, alog_ref, dtb_ref, nw_ref, wout_ref, o_ref,
                tail, qkv, qn, kn, state, y_scr, *, c, nchunk):
    nq, nv, dh = GDN_QK_HEADS, GDN_V_HEADS, GDN_D
    qk_w = nq * dh
    conv_ch = 2 * qk_w + nv * dh
    rep = nv // nq
    n = c // SUBLANES

    @pl.when(pl.program_id(1) == 0)
    def _():
        tail[...] = jnp.zeros_like(tail)
        state[...] = jnp.zeros_like(state)

    ri = lax.broadcasted_iota(jnp.int32, (c, c), 0)
    ci = lax.broadcasted_iota(jnp.int32, (c, c), 1)
    eye = ri == ci
    strict = ri > ci
    same = lambda sh: (ri >> sh) == (ci >> sh)
    levels = []
    sh = 4
    while (1 << sh) < c:
        levels.append(same(sh + 1) & jnp.logical_not(same(sh)) & strict)
        sh += 1
    diag_blocks = same(4) & strict
    plus_eye = lambda p: jnp.where(eye, 1.0, p).astype(BF16)
    heads = range(nv)

    def prep(q):
        ctx = dict(prods=[], x0b=[], f0b=[], attnb=[], rhsb=[], q_dec=[], k_decb=[], g_tot=[],
                   lowb=[[] for _ in levels])
        for k in range(conv_ch // LANES):
            xs = [z_ref[q, k, _slab_group(j, n), :] for j in range(n)]
            for j, xc in enumerate(_conv_steps(xs, cw_ref, tail, slice(k * LANES, (k + 1) * LANES))):
                qkv[q, k, _slab_group(j, n), :] = _silu(xc)
            yield
        zs = _slab_get(z_ref.at[q, (conv_ch + nv * dh) // LANES], n)
        beta_all = _sigmoid(zs)
        gcum = _cumsum_rows(-jnp.exp(alog_ref[...]) * _softplus(zs + dtb_ref[...]))
        gcum_t = gcum.T
        yield
        for j in range(nq):
            qj = _slab_get(qkv.at[q, j], n)
            kj = _slab_get(qkv.at[q, nq + j], n)
            qn[q, j] = qj * lax.rsqrt(jnp.sum(qj * qj, axis=-1, keepdims=True) + L2_EPS) * (dh ** -0.5)
            kn[q, j] = kj * lax.rsqrt(jnp.sum(kj * kj, axis=-1, keepdims=True) + L2_EPS)
            ctx["prods"].append(_dot_nt(jnp.concatenate([kn[q, j], qn[q, j]], axis=0).astype(BF16),
                                        kn[q, j].astype(BF16)))
            yield
        for i in heads:
            j = i // rep
            beta = beta_all[:, i:i + 1]
            g_col = gcum[:, nv + i:nv + i + 1]
            g_row = gcum_t[nv + i:nv + i + 1, :]
            g_end = g_col[c - 1:c, :]
            decay = jnp.where(ri >= ci, jnp.exp(jnp.minimum(g_col - g_row, 0.0)), 0.0)
            a_mat = beta * ctx["prods"][j][:c] * decay
            x0 = jnp.where(diag_blocks, -a_mat, 0.0)
            ctx["x0b"].append(x0.astype(BF16))
            ctx["f0b"].append(jnp.where(eye, 1.0, x0).astype(BF16))
            for lvl, mask in enumerate(levels):
                ctx["lowb"][lvl].append(jnp.where(mask, a_mat, 0.0).astype(BF16))
            ctx["attnb"].append((ctx["prods"][j][c:] * decay).astype(BF16))
            exp_g = jnp.exp(g_col)
            v = _slab_get(qkv.at[q, 2 * nq + i], n)
            ctx["rhsb"].append(jnp.concatenate([v * beta, kn[q, j] * (beta * exp_g)], axis=1).astype(BF16))
            ctx["q_dec"].append(qn[q, j] * exp_g)
            ctx["k_decb"].append((kn[q, j] * jnp.exp(g_end - g_col)).astype(BF16))
            ctx["g_tot"].append(jnp.exp(g_end))
            yield
        ctxs[q] = ctx

    def mxu(q):
        cx = ctxs[q]
        p1 = [_dot(x, x) for x in cx["x0b"]]
        yield
        m1 = [_dot(f0, plus_eye(p)).astype(BF16) for f0, p in zip(cx["f0b"], p1)]
        p1b = [p.astype(BF16) for p in p1]
        yield
        p2 = [_dot(p, p) for p in p1b]
        yield
        p2b = [p.astype(BF16) for p in p2]
        p3 = [_dot(pb, pb) for pb in p2b]
        yield
        m2 = [_dot(plus_eye(a), plus_eye(b_)).astype(BF16) for a, b_ in zip(p2, p3)]
        yield
        tb = [_dot(u, w).astype(BF16) for u, w in zip(m1, m2)]
        yield
        for lows in cx["lowb"]:
            corr = [jnp.where(eye, 1.0, -_dot(low, t)).astype(BF16) for low, t in zip(lows, tb)]
            yield
            tb = [_dot(t, g).astype(BF16) for t, g in zip(tb, corr)]
            yield
        uw = [_dot(t, r) for t, r in zip(tb, cx["rhsb"])]
        yield
        s_old = [state[i] for i in heads]
        wq = [_dot(jnp.concatenate([uw[i][:, dh:], cx["q_dec"][i]], axis=0).astype(BF16),
                   s_old[i].astype(BF16)) for i in heads]
        yield
        v_newb = [(uw[i][:, :dh] - wq[i][:c]).astype(BF16) for i in heads]
        cx["o"] = [wq[i][c:] + _dot(cx["attnb"][i], v_newb[i]) for i in heads]
        yield
        for i in heads:
            state[i] = s_old[i] * cx["g_tot"][i] + _dot_tn(cx["k_decb"][i], v_newb[i])
        yield

    def post(q):
        nw = nw_ref[...]
        for i in heads:
            zg = _slab_get(z_ref.at[q, conv_ch // LANES + i], n)
            y_scr[q * c:(q + 1) * c, i * dh:(i + 1) * dh] = (_rms(ctxs[q]["o"][i]) * nw * _silu(zg)).astype(BF16)
            yield

    ctxs = {}
    for _ in prep(0):
        pass
    for q in range(nchunk):
        side = itertools.chain(post(q - 1) if q > 0 else (), prep(q + 1) if q + 1 < nchunk else ())
        _interleave(mxu(q), side, GDN_SIDE_UNITS)
    for _ in post(nchunk - 1):
        pass
    o_ref[0] = h_ref[0] + _dot(y_scr[...], wout_ref[...])


def _gdn_layer(h, g, w_in, conv_w, a_log, dt_bias, norm_w, w_out, c=GDN_CHUNK, nchunk=GDN_CHUNKS_PER_STEP):
    b_, s_, d = h.shape
    v_w = w_out.shape[0]
    conv_ch = conv_w.shape[1]
    main_w = conv_ch + v_w
    small = w_in.shape[1] - main_w
    w_tail = jnp.pad(w_in[:, main_w:], ((0, 0), (0, LANES - small))).astype(BF16)
    n_cat = main_w + LANES
    z = _norm_matmul(h.reshape(b_ * s_, d), g, w_in[:, :main_w].astype(BF16), IN_PROJ_ROWS, slab_rows=c,
                     w_tail=w_tail)
    tt = c * nchunk
    nt = s_ // tt
    pad_row = lambda v: jnp.zeros((1, LANES), F32).at[0, GDN_V_HEADS:2 * GDN_V_HEADS].set(v)
    const2 = lambda bi, ti: (0, 0)
    rows = _slab_rows(c // SUBLANES)
    return pl.pallas_call(
        functools.partial(_gdn_kernel, c=c, nchunk=nchunk),
        grid=(b_, nt),
        in_specs=[pl.BlockSpec((nchunk, n_cat // LANES, rows, LANES), lambda bi, ti: (bi * nt + ti, 0, 0, 0)),
                  pl.BlockSpec((1, tt, d), lambda bi, ti: (bi, ti, 0)),
                  pl.BlockSpec((CONV_WIDTH, conv_ch), const2),
                  pl.BlockSpec((1, LANES), const2),
                  pl.BlockSpec((1, LANES), const2),
                  pl.BlockSpec((1, GDN_D), const2),
                  pl.BlockSpec((v_w, d), const2)],
        out_specs=pl.BlockSpec((1, tt, d), lambda bi, ti: (bi, ti, 0)),
        out_shape=jax.ShapeDtypeStruct((b_, s_, d), F32),
        scratch_shapes=[pltpu.VMEM((CONV_WIDTH - 1, SUBLANES, conv_ch), F32),
                        pltpu.VMEM((nchunk, conv_ch // LANES, rows, LANES), F32),
                        pltpu.VMEM((nchunk, GDN_QK_HEADS, c, GDN_D), F32),
                        pltpu.VMEM((nchunk, GDN_QK_HEADS, c, GDN_D), F32),
                        pltpu.VMEM((GDN_V_HEADS, GDN_D, GDN_D), F32),
                        pltpu.VMEM((tt, v_w), BF16)],
        compiler_params=_params(2),
        name="gdn_mixer",
    )(z, h, conv_w, pad_row(a_log), pad_row(dt_bias), norm_w.reshape(1, GDN_D),
      w_out.astype(BF16))


def kernel(x, mix_norm, mlp_norm, mlp_w_up, mlp_w_down, lru_w_in, lru_conv_w, lru_conv_b, lru_wa, lru_ba, lru_wx, lru_bx, lru_lambda, lru_w_out, ret_w_in, ret_w_out, gdn_w_in, gdn_conv_w, gdn_a_log, gdn_dt_bias, gdn_norm, gdn_w_out, final_norm):
    b_, s_, d = x.shape
    depth = mix_norm.shape[0]
    pos = jnp.arange(s_, dtype=F32)
    inv_freq = ROPE_BASE ** (-jnp.arange(0, RET_DK, 2, dtype=F32) / RET_DK)
    ang = pos[:, None] * inv_freq[None, :]
    cos, sin = jnp.cos(ang), jnp.sin(ang)
    lru_w = [w.astype(BF16) for w in (lru_w_in, lru_wa, lru_wx, lru_w_out)]
    h = x
    for i in range(depth):
        kind, j = i % 3, i // 3
        if kind == 0:
            h = _lru_layer(h, mix_norm[i], j, *lru_w, lru_conv_w[j], lru_conv_b[j],
                           lru_ba[j].reshape(-1), lru_bx[j].reshape(-1), lru_lambda[j])
        elif kind == 1:
            h = _ret_layer(h, mix_norm[i], ret_w_in[j], ret_w_out[j], cos, sin)
        else:
            h = _gdn_layer(h, mix_norm[i], gdn_w_in[j], gdn_conv_w[j], gdn_a_log[j],
                           gdn_dt_bias[j], gdn_norm[j], gdn_w_out[j])
        h = _mlp(h.reshape(b_ * s_, d), mlp_norm[i], mlp_w_up, mlp_w_down, i, final_norm, i == depth - 1,
                 MLP_ROWS, MLP_FF_BLOCK).reshape(b_, s_, d)
    return h
```

```python
import functools
import itertools
import math

import jax
import jax.numpy as jnp
from jax import lax
from jax.experimental import pallas as pl
from jax.experimental.pallas import tpu as pltpu

F32 = jnp.float32
BF16 = jnp.bfloat16

RMS_EPS = 1e-6
L2_EPS = 1e-6
CONV_WIDTH = 4
LRU_C = 8.0
LRU_BLOCK = 256
RET_HEADS = 4
RET_DK = 256
RET_DV = 512
ROPE_BASE = 10000.0
GDN_QK_HEADS = 8
GDN_V_HEADS = 16
GDN_D = 128
LANES = 128
SUBLANES = 8
VMEM_LIMIT = 56 * 1024 * 1024

IN_PROJ_ROWS = 512
LRU_ROWS = 512
RET_CHUNK = 256
RET_CHUNKS_PER_STEP = 2
GDN_CHUNK = 128
MLP_ROWS = 1024
MLP_FF_BLOCK = 1024

NT_DIMS = (((1,), (1,)), ((), ()))
TN_DIMS = (((0,), (0,)), ((), ()))


def _dot(a, b):
    return jnp.dot(a, b, preferred_element_type=F32)


def _dot_nt(a, b):
    return lax.dot_general(a, b, NT_DIMS, preferred_element_type=F32)


def _dot_tn(a, b):
    return lax.dot_general(a, b, TN_DIMS, preferred_element_type=F32)


def _rms(x):
    return x * lax.rsqrt(jnp.mean(x * x, axis=-1, keepdims=True) + RMS_EPS)


def _sigmoid(x):
    return 0.5 * jnp.tanh(0.5 * x) + 0.5


def _silu(x):
    half = 0.5 * x
    return half * jnp.tanh(half) + half


def _softplus(x):
    return jnp.maximum(x, 0.0) + jnp.log1p(jnp.exp(-jnp.abs(x)))


def _params(n_axes):
    return pltpu.CompilerParams(dimension_semantics=("arbitrary",) * n_axes,
                                vmem_limit_bytes=VMEM_LIMIT)


def _slab_rows(n):
    return SUBLANES * (n + 1)


def _slab_put(slab, x, n):
    for s in range(SUBLANES):
        slab[pl.ds(s * (n + 1), n), :] = x[s * n:(s + 1) * n, :]


def _slab_get(slab, n):
    return jnp.concatenate([slab[pl.ds(s * (n + 1), n), :] for s in range(SUBLANES)], axis=0)


def _slab_group(j, n):
    return pl.ds(j, SUBLANES, stride=n + 1)


def _norm_matmul_kernel(h_ref, g_ref, w_ref, *rest, slab_rows):
    o_ref = rest[-1]
    u = (_rms(h_ref[...]) * g_ref[...]).astype(BF16)
    z = _dot(u, w_ref[...])
    if slab_rows is None:
        o_ref[...] = z
    else:
        n = slab_rows // SUBLANES
        z2 = _dot(u, rest[0][...])
        nk = z.shape[1] // LANES
        for r in range(z.shape[0] // slab_rows):
            rows = slice(r * slab_rows, (r + 1) * slab_rows)
            for k in range(nk + 1):
                piece = z[rows, k * LANES:(k + 1) * LANES] if k < nk else z2[rows, :]
                _slab_put(o_ref.at[r, k], piece, n)
                o_ref[r, k, _slab_group(n, n), :] = jnp.zeros((SUBLANES, LANES), F32)


def _norm_matmul(h2, g, w, tm, slab_rows=None, w_tail=None, n=None):
    t, d = h2.shape
    n = w.shape[1] if n is None else n
    in_specs = [pl.BlockSpec((tm, d), lambda i: (i, 0)),
                pl.BlockSpec((1, d), lambda i: (0, 0)),
                pl.BlockSpec((d, n), lambda i: (0, 0), pipeline_mode=pl.Buffered(1))]
    args = [h2, g.reshape(1, d), w]
    if slab_rows is None:
        out_spec = pl.BlockSpec((tm, n), lambda i: (i, 0))
        out_shape = jax.ShapeDtypeStruct((t, n), F32)
    else:
        rows = _slab_rows(slab_rows // SUBLANES)
        nslab = n // LANES + 1
        in_specs.append(pl.BlockSpec((d, LANES), lambda i: (0, 0), pipeline_mode=pl.Buffered(1)))
        args.append(w_tail)
        out_spec = pl.BlockSpec((tm // slab_rows, nslab, rows, LANES), lambda i: (i, 0, 0, 0))
        out_shape = jax.ShapeDtypeStruct((t // slab_rows, nslab, rows, LANES), F32)
    return pl.pallas_call(
        functools.partial(_norm_matmul_kernel, slab_rows=slab_rows),
        grid=(t // tm,),
        in_specs=in_specs,
        out_specs=out_spec,
        out_shape=out_shape,
        compiler_params=_params(1),
        name="norm_matmul",
    )(*args)


def _mlp_kernel(h_ref, g_ref, wu_ref, wd_ref, fg_ref, o_ref, u_scr, *, final):
    j = pl.program_id(1)

    @pl.when(j == 0)
    def _():
        h = h_ref[...]
        u_scr[...] = (_rms(h) * g_ref[...]).astype(BF16)
        o_ref[...] = h

    hid = _dot(u_scr[...], wu_ref[...].astype(BF16))
    hid = jnp.square(jnp.maximum(hid, 0.0)).astype(BF16)
    o_ref[...] += _dot(hid, wd_ref[...].astype(BF16))

    if final:
        @pl.when(j == pl.num_programs(1) - 1)
        def _():
            o_ref[...] = _rms(o_ref[...]) * fg_ref[...]


def _mlp(h2, g, wu, wd, layer, fg, final, tm, tf):
    t, d = h2.shape
    ff = wu.shape[2]
    return pl.pallas_call(
        functools.partial(_mlp_kernel, final=final),
        grid=(t // tm, ff // tf),
        in_specs=[pl.BlockSpec((tm, d), lambda i, j: (i, 0)),
                  pl.BlockSpec((1, d), lambda i, j: (0, 0)),
                  pl.BlockSpec((None, d, tf), lambda i, j: (layer, 0, j)),
                  pl.BlockSpec((None, tf, d), lambda i, j: (layer, j, 0)),
                  pl.BlockSpec((1, d), lambda i, j: (0, 0))],
        out_specs=pl.BlockSpec((tm, d), lambda i, j: (i, 0)),
        out_shape=jax.ShapeDtypeStruct((t, d), F32),
        scratch_shapes=[pltpu.VMEM((tm, d), BF16)],
        compiler_params=_params(2),
        name="mlp",
    )(h2, g.reshape(1, d), wu, wd, fg.reshape(1, d))


def _conv_steps(xs, cw_ref, tail, lanes):
    n = len(xs)
    sub = lax.broadcasted_iota(jnp.int32, xs[0].shape, 0)
    hist = [pltpu.roll(jnp.where(sub == SUBLANES - 1, tail[k - 1, :, lanes], xs[n - k]), 1, 0)
            for k in range(1, CONV_WIDTH)]
    taps = [jnp.broadcast_to(cw_ref[k:k + 1, lanes], xs[0].shape) for k in range(CONV_WIDTH)]
    at = lambda j: xs[j] if j >= 0 else hist[-j - 1]
    out = []
    for j in range(n):
        acc = taps[CONV_WIDTH - 1] * xs[j]
        for k in range(1, CONV_WIDTH):
            acc = acc + taps[CONV_WIDTH - 1 - k] * at(j - k)
        out.append(acc)
    for k in range(1, CONV_WIDTH):
        tail[k - 1, :, lanes] = xs[n - k]
    return out


def _cumsum_rows(x):
    n = x.shape[0]
    row = lax.broadcasted_iota(jnp.int32, x.shape, 0)
    d = 1
    while d < n:
        x = x + jnp.where(row >= d, pltpu.roll(x, d, 0), 0.0)
        d *= 2
    return x


def _lru_kernel(h_ref, g_ref, win_ref, cw_ref, cb_ref, wa_ref, ba_ref, wx_ref, bx_ref, lam_ref,
                wout_ref, o_ref, tail, carry, x_slab, hs_slab, h0_scr, pp_scr, *, tt, width):
    n = tt // SUBLANES
    bw = LRU_BLOCK
    nblk = width // bw
    cpb = bw // LANES
    grp = lambda t, j: t[j * SUBLANES:(j + 1) * SUBLANES]
    grp_ds = lambda j: pl.ds(j * SUBLANES, SUBLANES)

    @pl.when(pl.program_id(1) == 0)
    def _():
        tail[...] = jnp.zeros_like(tail)
        carry[...] = jnp.zeros_like(carry)

    hn = h_ref[0]
    u = (_rms(hn) * g_ref[...]).astype(BF16)
    gate, out = {}, [hn]

    def x_proj(blk):
        zx = _dot(u, win_ref[:, width + blk * bw:width + (blk + 1) * bw])
        for c in range(cpb):
            _slab_put(x_slab.at[blk * cpb + c], zx[:, c * LANES:(c + 1) * LANES], n)

    def gate_proj(blk):
        gate[blk] = _dot(u, win_ref[:, blk * bw:(blk + 1) * bw])

    def recur(blk):
        lanes = slice(blk * bw, (blk + 1) * bw)
        xg = [jnp.concatenate(g, axis=1) for g in zip(*[
            _conv_steps([x_slab[blk * cpb + c, _slab_group(j, n), :] for j in range(n)],
                        cw_ref, tail, slice(blk * bw + c * LANES, blk * bw + (c + 1) * LANES))
            for c in range(cpb)])]
        cb = cb_ref[:, lanes]
        xg = [g + cb for g in xg]
        xb = jnp.concatenate(xg, axis=0).astype(BF16)
        ra = _dot(xb, wa_ref[blk])
        ia = _dot(xb, wx_ref[blk])
        ba = ba_ref[:, lanes]
        bx = bx_ref[:, lanes]
        sp = -LRU_C * _softplus(-lam_ref[:, lanes])
        hc = jnp.zeros((SUBLANES, bw), F32)
        pc = jnp.ones((SUBLANES, bw), F32)
        for j in range(n):
            r = _sigmoid(grp(ra, j) + ba)
            i = _sigmoid(grp(ia, j) + bx)
            aj = jnp.exp(r * sp)
            om = 1.0 - aj * aj
            bj = om * lax.rsqrt(jnp.maximum(om, 1e-30)) * (i * xg[j])
            hc = aj * hc + bj
            pc = aj * pc
            h0_scr[grp_ds(j), lanes] = hc
            pp_scr[grp_ds(j), lanes] = pc
        sub = lax.broadcasted_iota(jnp.int32, (SUBLANES, bw), 0)
        ea, eb = pc, hc
        d = 1
        while d < SUBLANES:
            keep = sub >= d
            eb = jnp.where(keep, ea * pltpu.roll(eb, d, 0) + eb, eb)
            ea = jnp.where(keep, ea * pltpu.roll(ea, d, 0), ea)
            d *= 2
        c0 = carry[:, lanes]
        seg_end = eb + ea * c0
        c_in = jnp.where(sub == 0, c0, pltpu.roll(seg_end, 1, 0))
        carry[:, lanes] = jnp.broadcast_to(seg_end[SUBLANES - 1:SUBLANES, :], (SUBLANES, bw))
        for j in range(n):
            hs = h0_scr[grp_ds(j), lanes] + pp_scr[grp_ds(j), lanes] * c_in
            for c in range(cpb):
                hs_slab[blk * cpb + c, _slab_group(j, n), :] = hs[:, c * LANES:(c + 1) * LANES]

    def out_piece(blk):
        hs_time = jnp.concatenate([_slab_get(hs_slab.at[blk * cpb + c], n) for c in range(cpb)], axis=1)
        y = (jax.nn.gelu(gate[blk]) * hs_time).astype(BF16)
        out[0] = out[0] + _dot(y, wout_ref[blk * bw:(blk + 1) * bw, :])

    x_proj(0)
    for blk in range(nblk):
        if blk + 1 < nblk:
            x_proj(blk + 1)
        gate_proj(blk)
        recur(blk)
        if blk > 0:
            out_piece(blk - 1)
    out_piece(nblk - 1)
    o_ref[0] = out[0]


def _lru_layer(h, g, layer, w_in, wa, wx, w_out, conv_w, conv_b, ba, bx, lam, tt=LRU_ROWS):
    b_, s_, d = h.shape
    width = w_out.shape[1]
    row = lambda v: v.reshape(1, width)
    const2 = lambda bi, ti: (0, 0)
    slab = pltpu.VMEM((width // LANES, _slab_rows(tt // SUBLANES), LANES), F32)
    return pl.pallas_call(
        functools.partial(_lru_kernel, tt=tt, width=width),
        grid=(b_, s_ // tt),
        in_specs=[pl.BlockSpec((1, tt, d), lambda bi, ti: (bi, ti, 0)),
                  pl.BlockSpec((1, d), const2),
                  pl.BlockSpec((None, d, 2 * width), lambda bi, ti: (layer, 0, 0)),
                  pl.BlockSpec((CONV_WIDTH, width), const2),
                  pl.BlockSpec((1, width), const2),
                  pl.BlockSpec((None,) + wa.shape[1:], lambda bi, ti: (layer, 0, 0, 0)),
                  pl.BlockSpec((1, width), const2),
                  pl.BlockSpec((None,) + wx.shape[1:], lambda bi, ti: (layer, 0, 0, 0)),
                  pl.BlockSpec((1, width), const2),
                  pl.BlockSpec((1, width), const2),
                  pl.BlockSpec((None, width, d), lambda bi, ti: (layer, 0, 0))],
        out_specs=pl.BlockSpec((1, tt, d), lambda bi, ti: (bi, ti, 0)),
        out_shape=jax.ShapeDtypeStruct((b_, s_, d), F32),
        scratch_shapes=[pltpu.VMEM((CONV_WIDTH - 1, SUBLANES, width), F32),
                        pltpu.VMEM((SUBLANES, width), F32), slab, slab,
                        pltpu.VMEM((tt, width), F32), pltpu.VMEM((tt, width), F32)],
        compiler_params=_params(2),
        name="lru_mixer",
    )(h, g.reshape(1, d), w_in, conv_w, row(conv_b), wa, row(ba), wx, row(bx), row(lam), w_out)


def _rope(x, cos, sin):
    half = x.shape[1] // 2
    x1, x2 = x[:, :half], x[:, half:]
    return jnp.concatenate([x1 * cos - x2 * sin, x2 * cos + x1 * sin], axis=1)


def _ret_kernel(z_ref, h_ref, cos_ref, sin_ref, wout_ref, o_ref, state, y_scr, dmask_scr, *, c, nchunk):
    log_gammas = [math.log1p(-(2.0 ** (-5.0 - hd))) for hd in range(RET_HEADS)]

    @pl.when(pl.program_id(1) == 0)
    def _():
        state[...] = jnp.zeros_like(state)
        ri = lax.broadcasted_iota(jnp.int32, (c, c), 0)
        ci = lax.broadcasted_iota(jnp.int32, (c, c), 1)
        diff = (ri - ci).astype(F32)
        for hd in range(RET_HEADS):
            dmask_scr[hd] = jnp.where(diff >= 0.0, jnp.exp(log_gammas[hd] * jnp.maximum(diff, 0.0)), 0.0)

    pos = lax.broadcasted_iota(jnp.int32, (c, 1), 0).astype(F32)
    qk_w = RET_HEADS * RET_DK
    v_w = RET_HEADS * RET_DV
    for q_ in range(nchunk):
        rows = slice(q_ * c, (q_ + 1) * c)
        cos = cos_ref[rows, :]
        sin = sin_ref[rows, :]
        for hd in range(RET_HEADS):
            log_gamma = log_gammas[hd]
            q = _rope(z_ref[0, rows, hd * RET_DK:(hd + 1) * RET_DK], cos, sin)
            k = _rope(z_ref[0, rows, qk_w + hd * RET_DK:qk_w + (hd + 1) * RET_DK], cos, sin) * (RET_DK ** -0.5)
            v = z_ref[0, rows, 2 * qk_w + hd * RET_DV:2 * qk_w + (hd + 1) * RET_DV].astype(BF16)
            gt = z_ref[0, rows, 2 * qk_w + v_w + hd * RET_DV:2 * qk_w + v_w + (hd + 1) * RET_DV]
            inter_dec = jnp.exp(log_gamma * (pos + 1.0))
            k_dec = jnp.exp(log_gamma * (c - 1.0 - pos))
            chunk_dec = math.exp(log_gamma * c)
            qb = q.astype(BF16)
            scores = _dot_nt(qb, k.astype(BF16)) * dmask_scr[hd]
            s_old = state[hd]
            o = _dot(scores.astype(BF16), v) + _dot(qb, s_old.astype(BF16)) * inter_dec
            state[hd] = s_old * chunk_dec + _dot_tn((k * k_dec).astype(BF16), v)
            y = _silu(gt) * _rms(o)
            y_scr[rows, hd * RET_DV:(hd + 1) * RET_DV] = y.astype(BF16)
    o_ref[0] = h_ref[0] + _dot(y_scr[...], wout_ref[...])


def _ret_layer(h, g, w_in, w_out, cos, sin, c=RET_CHUNK, nchunk=RET_CHUNKS_PER_STEP):
    b_, s_, d = h.shape
    n_in = w_in.shape[1]
    v_w = w_out.shape[0]
    tt = c * nchunk
    z = _norm_matmul(h.reshape(b_ * s_, d), g, w_in.astype(BF16), IN_PROJ_ROWS).reshape(b_, s_, n_in)
    return pl.pallas_call(
        functools.partial(_ret_kernel, c=c, nchunk=nchunk),
        grid=(b_, s_ // tt),
        in_specs=[pl.BlockSpec((1, tt, n_in), lambda bi, ti: (bi, ti, 0)),
                  pl.BlockSpec((1, tt, d), lambda bi, ti: (bi, ti, 0)),
                  pl.BlockSpec((tt, RET_DK // 2), lambda bi, ti: (ti, 0)),
                  pl.BlockSpec((tt, RET_DK // 2), lambda bi, ti: (ti, 0)),
                  pl.BlockSpec((v_w, d), lambda bi, ti: (0, 0), pipeline_mode=pl.Buffered(1))],
        out_specs=pl.BlockSpec((1, tt, d), lambda bi, ti: (bi, ti, 0)),
        out_shape=jax.ShapeDtypeStruct((b_, s_, d), F32),
        scratch_shapes=[pltpu.VMEM((RET_HEADS, RET_DK, RET_DV), F32),
                        pltpu.VMEM((tt, v_w), BF16),
                        pltpu.VMEM((RET_HEADS, c, c), F32)],
        compiler_params=_params(2),
        name="retention_mixer",
    )(z, h, cos, sin, w_out.astype(BF16))


GDN_CHUNKS_PER_STEP = 2
GDN_SIDE_UNITS = 3


def _interleave(main, side, per_step):
    for _ in main:
        for _ in range(per_step):
            next(side, None)
    for _ in side:
        pass


def _gdn_kernel(z_ref, h_ref, cw_ref, alog_ref, dtb_ref, nw_ref, wout_ref, o_ref,
                tail, qkv, qn, kn, state, y_scr, *, c, nchunk):
    nq, nv, dh = GDN_QK_HEADS, GDN_V_HEADS, GDN_D
    qk_w = nq * dh
    conv_ch = 2 * qk_w + nv * dh
    rep = nv // nq
    n = c // SUBLANES

    @pl.when(pl.program_id(1) == 0)
    def _():
        tail[...] = jnp.zeros_like(tail)
        state[...] = jnp.zeros_like(state)

    ri = lax.broadcasted_iota(jnp.int32, (c, c), 0)
    ci = lax.broadcasted_iota(jnp.int32, (c, c), 1)
    eye = ri == ci
    strict = ri > ci
    same = lambda sh: (ri >> sh) == (ci >> sh)
    levels = []
    sh = 4
    while (1 << sh) < c:
        levels.append(same(sh + 1) & jnp.logical_not(same(sh)) & strict)
        sh += 1
    diag_blocks = same(4) & strict
    plus_eye = lambda p: jnp.where(eye, 1.0, p).astype(BF16)
    heads = range(nv)

    def prep(q):
        ctx = dict(prods=[], x0b=[], f0b=[], attnb=[], rhsb=[], q_dec=[], k_decb=[], g_tot=[],
                   lowb=[[] for _ in levels])
        for k in range(conv_ch // LANES):
            xs = [z_ref[q, k, _slab_group(j, n), :] for j in range(n)]
            for j, xc in enumerate(_conv_steps(xs, cw_ref, tail, slice(k * LANES, (k + 1) * LANES))):
                qkv[q, k, _slab_group(j, n), :] = _silu(xc)
            yield
        zs = _slab_get(z_ref.at[q, (conv_ch + nv * dh) // LANES], n)
        beta_all = _sigmoid(zs)
        gcum = _cumsum_rows(-jnp.exp(alog_ref[...]) * _softplus(zs + dtb_ref[...]))
        gcum_t = gcum.T
        yield
        for j in range(nq):
            qj = _slab_get(qkv.at[q, j], n)
            kj = _slab_get(qkv.at[q, nq + j], n)
            qn[q, j] = qj * lax.rsqrt(jnp.sum(qj * qj, axis=-1, keepdims=True) + L2_EPS) * (dh ** -0.5)
            kn[q, j] = kj * lax.rsqrt(jnp.sum(kj * kj, axis=-1, keepdims=True) + L2_EPS)
            ctx["prods"].append(_dot_nt(jnp.concatenate([kn[q, j], qn[q, j]], axis=0).astype(BF16),
                                        kn[q, j].astype(BF16)))
            yield
        for i in heads:
            j = i // rep
            beta = beta_all[:, i:i + 1]
            g_col = gcum[:, nv + i:nv + i + 1]
            g_row = gcum_t[nv + i:nv + i + 1, :]
            g_end = g_col[c - 1:c, :]
            decay = jnp.where(ri >= ci, jnp.exp(jnp.minimum(g_col - g_row, 0.0)), 0.0)
            a_mat = beta * ctx["prods"][j][:c] * decay
            x0 = jnp.where(diag_blocks, -a_mat, 0.0)
            ctx["x0b"].append(x0.astype(BF16))
            ctx["f0b"].append(jnp.where(eye, 1.0, x0).astype(BF16))
            for lvl, mask in enumerate(levels):
                ctx["lowb"][lvl].append(jnp.where(mask, a_mat, 0.0).astype(BF16))
            ctx["attnb"].append((ctx["prods"][j][c:] * decay).astype(BF16))
            exp_g = jnp.exp(g_col)
            v = _slab_get(qkv.at[q, 2 * nq + i], n)
            ctx["rhsb"].append(jnp.concatenate([v * beta, kn[q, j] * (beta * exp_g)], axis=1).astype(BF16))
            ctx["q_dec"].append(qn[q, j] * exp_g)
            ctx["k_decb"].append((kn[q, j] * jnp.exp(g_end - g_col)).astype(BF16))
            ctx["g_tot"].append(jnp.exp(g_end))
            yield
        ctxs[q] = ctx

    def mxu(q):
        cx = ctxs[q]
        p1 = [_dot(x, x) for x in cx["x0b"]]
        yield
        m1 = [_dot(f0, plus_eye(p)).astype(BF16) for f0, p in zip(cx["f0b"], p1)]
        p1b = [p.astype(BF16) for p in p1]
        yield
        p2 = [_dot(p, p) for p in p1b]
        yield
        p2b = [p.astype(BF16) for p in p2]
        p3 = [_dot(pb, pb) for pb in p2b]
        yield
        m2 = [_dot(plus_eye(a), plus_eye(b_)).astype(BF16) for a, b_ in zip(p2, p3)]
        yield
        tb = [_dot(u, w).astype(BF16) for u, w in zip(m1, m2)]
        yield
        for lows in cx["lowb"]:
            corr = [jnp.where(eye, 1.0, -_dot(low, t)).astype(BF16) for low, t in zip(lows, tb)]
            yield
            tb = [_dot(t, g).astype(BF16) for t, g in zip(tb, corr)]
            yield
        uw = [_dot(t, r) for t, r in zip(tb, cx["rhsb"])]
        yield
        s_old = [state[i] for i in heads]
        wq = [_dot(jnp.concatenate([uw[i][:, dh:], cx["q_dec"][i]], axis=0).astype(BF16),
                   s_old[i].astype(BF16)) for i in heads]
        yield
        v_newb = [(uw[i][:, :dh] - wq[i][:c]).astype(BF16) for i in heads]
        cx["o"] = [wq[i][c:] + _dot(cx["attnb"][i], v_newb[i]) for i in heads]
        yield
        for i in heads:
            state[i] = s_old[i] * cx["g_tot"][i] + _dot_tn(cx["k_decb"][i], v_newb[i])
        yield

    def post(q):
        nw = nw_ref[...]
        for i in heads:
            zg = _slab_get(z_ref.at[q, conv_ch // LANES + i], n)
            y_scr[q * c:(q + 1) * c, i * dh:(i + 1) * dh] = (_rms(ctxs[q]["o"][i]) * nw * _silu(zg)).astype(BF16)
            yield

    ctxs = {}
    for _ in prep(0):
        pass
    for q in range(nchunk):
        side = itertools.chain(post(q - 1) if q > 0 else (), prep(q + 1) if q + 1 < nchunk else ())
        _interleave(mxu(q), side, GDN_SIDE_UNITS)
    for _ in post(nchunk - 1):
        pass
    o_ref[0] = h_ref[0] + _dot(y_scr[...], wout_ref[...])


def _gdn_layer(h, g, w_in, conv_w, a_log, dt_bias, norm_w, w_out, c=GDN_CHUNK, nchunk=GDN_CHUNKS_PER_STEP):
    b_, s_, d = h.shape
    v_w = w_out.shape[0]
    conv_ch = conv_w.shape[1]
    main_w = conv_ch + v_w
    small = w_in.shape[1] - main_w
    w_tail = jnp.pad(w_in[:, main_w:], ((0, 0), (0, LANES - small))).astype(BF16)
    n_cat = main_w + LANES
    z = _norm_matmul(h.reshape(b_ * s_, d), g, w_in.astype(BF16), IN_PROJ_ROWS, slab_rows=c,
                     w_tail=w_tail, n=main_w)
    tt = c * nchunk
    nt = s_ // tt
    pad_row = lambda v: jnp.zeros((1, LANES), F32).at[0, GDN_V_HEADS:2 * GDN_V_HEADS].set(v)
    const2 = lambda bi, ti: (0, 0)
    rows = _slab_rows(c // SUBLANES)
    return pl.pallas_call(
        functools.partial(_gdn_kernel, c=c, nchunk=nchunk),
        grid=(b_, nt),
        in_specs=[pl.BlockSpec((nchunk, n_cat // LANES, rows, LANES), lambda bi, ti: (bi * nt + ti, 0, 0, 0)),
                  pl.BlockSpec((1, tt, d), lambda bi, ti: (bi, ti, 0)),
                  pl.BlockSpec((CONV_WIDTH, conv_ch), const2),
                  pl.BlockSpec((1, LANES), const2),
                  pl.BlockSpec((1, LANES), const2),
                  pl.BlockSpec((1, GDN_D), const2),
                  pl.BlockSpec((v_w, d), const2)],
        out_specs=pl.BlockSpec((1, tt, d), lambda bi, ti: (bi, ti, 0)),
        out_shape=jax.ShapeDtypeStruct((b_, s_, d), F32),
        scratch_shapes=[pltpu.VMEM((CONV_WIDTH - 1, SUBLANES, conv_ch), F32),
                        pltpu.VMEM((nchunk, conv_ch // LANES, rows, LANES), F32),
                        pltpu.VMEM((nchunk, GDN_QK_HEADS, c, GDN_D), F32),
                        pltpu.VMEM((nchunk, GDN_QK_HEADS, c, GDN_D), F32),
                        pltpu.VMEM((GDN_V_HEADS, GDN_D, GDN_D), F32),
                        pltpu.VMEM((tt, v_w), BF16)],
        compiler_params=_params(2),
        name="gdn_mixer",
    )(z, h, conv_w, pad_row(a_log), pad_row(dt_bias), norm_w.reshape(1, GDN_D),
      w_out.astype(BF16))


def kernel(x, mix_norm, mlp_norm, mlp_w_up, mlp_w_down, lru_w_in, lru_conv_w, lru_conv_b, lru_wa, lru_ba, lru_wx, lru_bx, lru_lambda, lru_w_out, ret_w_in, ret_w_out, gdn_w_in, gdn_conv_w, gdn_a_log, gdn_dt_bias, gdn_norm, gdn_w_out, final_norm):
    b_, s_, d = x.shape
    depth = mix_norm.shape[0]
    pos = jnp.arange(s_, dtype=F32)
    inv_freq = ROPE_BASE ** (-jnp.arange(0, RET_DK, 2, dtype=F32) / RET_DK)
    ang = pos[:, None] * inv_freq[None, :]
    cos, sin = jnp.cos(ang), jnp.sin(ang)
    lru_w = [w.astype(BF16) for w in (lru_w_in, lru_wa, lru_wx, lru_w_out)]
    h = x
    for i in range(depth):
        kind, j = i % 3, i // 3
        if kind == 0:
            h = _lru_layer(h, mix_norm[i], j, *lru_w, lru_conv_w[j], lru_conv_b[j],
                           lru_ba[j].reshape(-1), lru_bx[j].reshape(-1), lru_lambda[j])
        elif kind == 1:
            h = _ret_layer(h, mix_norm[i], ret_w_in[j], ret_w_out[j], cos, sin)
        else:
            h = _gdn_layer(h, mix_norm[i], gdn_w_in[j], gdn_conv_w[j], gdn_a_log[j],
                           gdn_dt_bias[j], gdn_norm[j], gdn_w_out[j])
        h = _mlp(h.reshape(b_ * s_, d), mlp_norm[i], mlp_w_up, mlp_w_down, i, final_norm, i == depth - 1,
                 MLP_ROWS, MLP_FF_BLOCK).reshape(b_, s_, d)
    return h
```

```python
import functools
import itertools
import math

import jax
import jax.numpy as jnp
from jax import lax
from jax.experimental import pallas as pl
from jax.experimental.pallas import tpu as pltpu

F32 = jnp.float32
BF16 = jnp.bfloat16

RMS_EPS = 1e-6
L2_EPS = 1e-6
CONV_WIDTH = 4
LRU_C = 8.0
LRU_BLOCK = 256
RET_HEADS = 4
RET_DK = 256
RET_DV = 512
ROPE_BASE = 10000.0
GDN_QK_HEADS = 8
GDN_V_HEADS = 16
GDN_D = 128
LANES = 128
SUBLANES = 8
V7X_VMEM_BYTES = 64 * 1024 * 1024
VMEM_LIMIT = V7X_VMEM_BYTES * 7 // 8

IN_PROJ_ROWS = 512
LRU_ROWS = 512
RET_CHUNK = 256
RET_CHUNKS_PER_STEP = 2
GDN_CHUNK = 128
MLP_ROWS = 1024
MLP_FF_BLOCK = 1024

NT_DIMS = (((1,), (1,)), ((), ()))
TN_DIMS = (((0,), (0,)), ((), ()))


def _dot(a, b):
    return jnp.dot(a, b, preferred_element_type=F32)


def _dot_nt(a, b):
    return lax.dot_general(a, b, NT_DIMS, preferred_element_type=F32)


def _dot_tn(a, b):
    return lax.dot_general(a, b, TN_DIMS, preferred_element_type=F32)


def _rms(x):
    return x * lax.rsqrt(jnp.mean(x * x, axis=-1, keepdims=True) + RMS_EPS)


def _sigmoid(x):
    return 0.5 * jnp.tanh(0.5 * x) + 0.5


def _silu(x):
    half = 0.5 * x
    return half * jnp.tanh(half) + half


def _softplus(x):
    return jnp.maximum(x, 0.0) + jnp.log1p(jnp.exp(-jnp.abs(x)))


def _params(n_axes):
    return pltpu.CompilerParams(dimension_semantics=("arbitrary",) * n_axes,
                                vmem_limit_bytes=VMEM_LIMIT)


def _slab_rows(n):
    return SUBLANES * (n + 1)


def _slab_put(slab, x, n):
    for s in range(SUBLANES):
        slab[pl.ds(s * (n + 1), n), :] = x[s * n:(s + 1) * n, :]


def _slab_get(slab, n):
    return jnp.concatenate([slab[pl.ds(s * (n + 1), n), :] for s in range(SUBLANES)], axis=0)


def _slab_group(j, n):
    return pl.ds(j, SUBLANES, stride=n + 1)


def _norm_matmul_kernel(h_ref, g_ref, w_ref, *rest, slab_rows):
    o_ref = rest[-1]
    u = (_rms(h_ref[...]) * g_ref[...]).astype(BF16)
    z = _dot(u, w_ref[...])
    if slab_rows is None:
        o_ref[...] = z
    else:
        n = slab_rows // SUBLANES
        z2 = _dot(u, rest[0][...])
        nk = z.shape[1] // LANES
        for r in range(z.shape[0] // slab_rows):
            rows = slice(r * slab_rows, (r + 1) * slab_rows)
            for k in range(nk + 1):
                piece = z[rows, k * LANES:(k + 1) * LANES] if k < nk else z2[rows, :]
                _slab_put(o_ref.at[r, k], piece, n)
                o_ref[r, k, _slab_group(n, n), :] = jnp.zeros((SUBLANES, LANES), F32)


def _norm_matmul(h2, g, w, tm, slab_rows=None, w_tail=None, n=None):
    t, d = h2.shape
    n = w.shape[1] if n is None else n
    in_specs = [pl.BlockSpec((tm, d), lambda i: (i, 0)),
                pl.BlockSpec((1, d), lambda i: (0, 0)),
                pl.BlockSpec((d, n), lambda i: (0, 0), pipeline_mode=pl.Buffered(1))]
    args = [h2, g.reshape(1, d), w]
    if slab_rows is None:
        out_spec = pl.BlockSpec((tm, n), lambda i: (i, 0))
        out_shape = jax.ShapeDtypeStruct((t, n), F32)
    else:
        rows = _slab_rows(slab_rows // SUBLANES)
        nslab = n // LANES + 1
        in_specs.append(pl.BlockSpec((d, LANES), lambda i: (0, 0), pipeline_mode=pl.Buffered(1)))
        args.append(w_tail)
        out_spec = pl.BlockSpec((tm // slab_rows, nslab, rows, LANES), lambda i: (i, 0, 0, 0))
        out_shape = jax.ShapeDtypeStruct((t // slab_rows, nslab, rows, LANES), F32)
    return pl.pallas_call(
        functools.partial(_norm_matmul_kernel, slab_rows=slab_rows),
        grid=(t // tm,),
        in_specs=in_specs,
        out_specs=out_spec,
        out_shape=out_shape,
        compiler_params=_params(1),
        name="norm_matmul",
    )(*args)


def _mlp_kernel(h_ref, g_ref, wu_ref, wd_ref, fg_ref, o_ref, u_scr, *, final):
    j = pl.program_id(1)

    @pl.when(j == 0)
    def _():
        h = h_ref[...]
        u_scr[...] = (_rms(h) * g_ref[...]).astype(BF16)
        o_ref[...] = h

    hid = _dot(u_scr[...], wu_ref[...].astype(BF16))
    hid = jnp.square(jnp.maximum(hid, 0.0)).astype(BF16)
    o_ref[...] += _dot(hid, wd_ref[...].astype(BF16))

    if final:
        @pl.when(j == pl.num_programs(1) - 1)
        def _():
            o_ref[...] = _rms(o_ref[...]) * fg_ref[...]


def _mlp(h2, g, wu, wd, layer, fg, final, tm, tf):
    t, d = h2.shape
    ff = wu.shape[2]
    return pl.pallas_call(
        functools.partial(_mlp_kernel, final=final),
        grid=(t // tm, ff // tf),
        in_specs=[pl.BlockSpec((tm, d), lambda i, j: (i, 0)),
                  pl.BlockSpec((1, d), lambda i, j: (0, 0)),
                  pl.BlockSpec((None, d, tf), lambda i, j: (layer, 0, j)),
                  pl.BlockSpec((None, tf, d), lambda i, j: (layer, j, 0)),
                  pl.BlockSpec((1, d), lambda i, j: (0, 0))],
        out_specs=pl.BlockSpec((tm, d), lambda i, j: (i, 0)),
        out_shape=jax.ShapeDtypeStruct((t, d), F32),
        scratch_shapes=[pltpu.VMEM((tm, d), BF16)],
        compiler_params=_params(2),
        name="mlp",
    )(h2, g.reshape(1, d), wu, wd, fg.reshape(1, d))


def _conv_steps(xs, cw_ref, tail, lanes):
    n = len(xs)
    sub = lax.broadcasted_iota(jnp.int32, xs[0].shape, 0)
    hist = [pltpu.roll(jnp.where(sub == SUBLANES - 1, tail[k - 1, :, lanes], xs[n - k]), 1, 0)
            for k in range(1, CONV_WIDTH)]
    taps = [jnp.broadcast_to(cw_ref[k:k + 1, lanes], xs[0].shape) for k in range(CONV_WIDTH)]
    at = lambda j: xs[j] if j >= 0 else hist[-j - 1]
    out = []
    for j in range(n):
        acc = taps[CONV_WIDTH - 1] * xs[j]
        for k in range(1, CONV_WIDTH):
            acc = acc + taps[CONV_WIDTH - 1 - k] * at(j - k)
        out.append(acc)
    for k in range(1, CONV_WIDTH):
        tail[k - 1, :, lanes] = xs[n - k]
    return out


def _cumsum_rows(x):
    n = x.shape[0]
    row = lax.broadcasted_iota(jnp.int32, x.shape, 0)
    d = 1
    while d < n:
        x = x + jnp.where(row >= d, pltpu.roll(x, d, 0), 0.0)
        d *= 2
    return x


def _lru_kernel(h_ref, g_ref, win_ref, cw_ref, cb_ref, wa_ref, ba_ref, wx_ref, bx_ref, lam_ref,
                wout_ref, o_ref, tail, carry, x_slab, hs_slab, h0_scr, pp_scr, *, tt, width):
    n = tt // SUBLANES
    bw = LRU_BLOCK
    nblk = width // bw
    cpb = bw // LANES
    grp = lambda t, j: t[j * SUBLANES:(j + 1) * SUBLANES]
    grp_ds = lambda j: pl.ds(j * SUBLANES, SUBLANES)

    @pl.when(pl.program_id(1) == 0)
    def _():
        tail[...] = jnp.zeros_like(tail)
        carry[...] = jnp.zeros_like(carry)

    hn = h_ref[0]
    u = (_rms(hn) * g_ref[...]).astype(BF16)
    gate, out = {}, [hn]

    def x_proj(blk):
        zx = _dot(u, win_ref[:, width + blk * bw:width + (blk + 1) * bw])
        for c in range(cpb):
            _slab_put(x_slab.at[blk * cpb + c], zx[:, c * LANES:(c + 1) * LANES], n)

    def gate_proj(blk):
        gate[blk] = _dot(u, win_ref[:, blk * bw:(blk + 1) * bw])

    def recur(blk):
        lanes = slice(blk * bw, (blk + 1) * bw)
        xg = [jnp.concatenate(g, axis=1) for g in zip(*[
            _conv_steps([x_slab[blk * cpb + c, _slab_group(j, n), :] for j in range(n)],
                        cw_ref, tail, slice(blk * bw + c * LANES, blk * bw + (c + 1) * LANES))
            for c in range(cpb)])]
        cb = cb_ref[:, lanes]
        xg = [g + cb for g in xg]
        xb = jnp.concatenate(xg, axis=0).astype(BF16)
        ra = _dot(xb, wa_ref[blk])
        ia = _dot(xb, wx_ref[blk])
        ba = ba_ref[:, lanes]
        bx = bx_ref[:, lanes]
        sp = -LRU_C * _softplus(-lam_ref[:, lanes])
        hc = jnp.zeros((SUBLANES, bw), F32)
        pc = jnp.ones((SUBLANES, bw), F32)
        for j in range(n):
            r = _sigmoid(grp(ra, j) + ba)
            i = _sigmoid(grp(ia, j) + bx)
            aj = jnp.exp(r * sp)
            om = 1.0 - aj * aj
            bj = om * lax.rsqrt(jnp.maximum(om, 1e-30)) * (i * xg[j])
            hc = aj * hc + bj
            pc = aj * pc
            h0_scr[grp_ds(j), lanes] = hc
            pp_scr[grp_ds(j), lanes] = pc
        sub = lax.broadcasted_iota(jnp.int32, (SUBLANES, bw), 0)
        ea, eb = pc, hc
        d = 1
        while d < SUBLANES:
            keep = sub >= d
            eb = jnp.where(keep, ea * pltpu.roll(eb, d, 0) + eb, eb)
            ea = jnp.where(keep, ea * pltpu.roll(ea, d, 0), ea)
            d *= 2
        c0 = carry[:, lanes]
        seg_end = eb + ea * c0
        c_in = jnp.where(sub == 0, c0, pltpu.roll(seg_end, 1, 0))
        carry[:, lanes] = jnp.broadcast_to(seg_end[SUBLANES - 1:SUBLANES, :], (SUBLANES, bw))
        for j in range(n):
            hs = h0_scr[grp_ds(j), lanes] + pp_scr[grp_ds(j), lanes] * c_in
            for c in range(cpb):
                hs_slab[blk * cpb + c, _slab_group(j, n), :] = hs[:, c * LANES:(c + 1) * LANES]

    def out_piece(blk):
        hs_time = jnp.concatenate([_slab_get(hs_slab.at[blk * cpb + c], n) for c in range(cpb)], axis=1)
        y = (jax.nn.gelu(gate[blk]) * hs_time).astype(BF16)
        out[0] = out[0] + _dot(y, wout_ref[blk * bw:(blk + 1) * bw, :])

    x_proj(0)
    for blk in range(nblk):
        if blk + 1 < nblk:
            x_proj(blk + 1)
        gate_proj(blk)
        recur(blk)
        if blk > 0:
            out_piece(blk - 1)
    out_piece(nblk - 1)
    o_ref[0] = out[0]


def _lru_layer(h, g, layer, w_in, wa, wx, w_out, conv_w, conv_b, ba, bx, lam, tt=LRU_ROWS):
    b_, s_, d = h.shape
    width = w_out.shape[1]
    row = lambda v: v.reshape(1, width)
    const2 = lambda bi, ti: (0, 0)
    slab = pltpu.VMEM((width // LANES, _slab_rows(tt // SUBLANES), LANES), F32)
    return pl.pallas_call(
        functools.partial(_lru_kernel, tt=tt, width=width),
        grid=(b_, s_ // tt),
        in_specs=[pl.BlockSpec((1, tt, d), lambda bi, ti: (bi, ti, 0)),
                  pl.BlockSpec((1, d), const2),
                  pl.BlockSpec((None, d, 2 * width), lambda bi, ti: (layer, 0, 0)),
                  pl.BlockSpec((CONV_WIDTH, width), const2),
                  pl.BlockSpec((1, width), const2),
                  pl.BlockSpec((None,) + wa.shape[1:], lambda bi, ti: (layer, 0, 0, 0)),
                  pl.BlockSpec((1, width), const2),
                  pl.BlockSpec((None,) + wx.shape[1:], lambda bi, ti: (layer, 0, 0, 0)),
                  pl.BlockSpec((1, width), const2),
                  pl.BlockSpec((1, width), const2),
                  pl.BlockSpec((None, width, d), lambda bi, ti: (layer, 0, 0))],
        out_specs=pl.BlockSpec((1, tt, d), lambda bi, ti: (bi, ti, 0)),
        out_shape=jax.ShapeDtypeStruct((b_, s_, d), F32),
        scratch_shapes=[pltpu.VMEM((CONV_WIDTH - 1, SUBLANES, width), F32),
                        pltpu.VMEM((SUBLANES, width), F32), slab, slab,
                        pltpu.VMEM((tt, width), F32), pltpu.VMEM((tt, width), F32)],
        compiler_params=_params(2),
        name="lru_mixer",
    )(h, g.reshape(1, d), w_in, conv_w, row(conv_b), wa, row(ba), wx, row(bx), row(lam), w_out)


def _rope(x, cos, sin):
    half = x.shape[1] // 2
    x1, x2 = x[:, :half], x[:, half:]
    return jnp.concatenate([x1 * cos - x2 * sin, x2 * cos + x1 * sin], axis=1)


def _ret_kernel(z_ref, h_ref, cos_ref, sin_ref, wout_ref, o_ref, state, y_scr, dmask_scr, *, c, nchunk):
    log_gammas = [math.log1p(-(2.0 ** (-5.0 - hd))) for hd in range(RET_HEADS)]

    @pl.when(pl.program_id(1) == 0)
    def _():
        state[...] = jnp.zeros_like(state)
        ri = lax.broadcasted_iota(jnp.int32, (c, c), 0)
        ci = lax.broadcasted_iota(jnp.int32, (c, c), 1)
        diff = (ri - ci).astype(F32)
        for hd in range(RET_HEADS):
            dmask_scr[hd] = jnp.where(diff >= 0.0, jnp.exp(log_gammas[hd] * jnp.maximum(diff, 0.0)), 0.0)

    pos = lax.broadcasted_iota(jnp.int32, (c, 1), 0).astype(F32)
    qk_w = RET_HEADS * RET_DK
    v_w = RET_HEADS * RET_DV
    for q_ in range(nchunk):
        rows = slice(q_ * c, (q_ + 1) * c)
        cos = cos_ref[rows, :]
        sin = sin_ref[rows, :]
        for hd in range(RET_HEADS):
            log_gamma = log_gammas[hd]
            q = _rope(z_ref[0, rows, hd * RET_DK:(hd + 1) * RET_DK], cos, sin)
            k = _rope(z_ref[0, rows, qk_w + hd * RET_DK:qk_w + (hd + 1) * RET_DK], cos, sin) * (RET_DK ** -0.5)
            v = z_ref[0, rows, 2 * qk_w + hd * RET_DV:2 * qk_w + (hd + 1) * RET_DV].astype(BF16)
            gt = z_ref[0, rows, 2 * qk_w + v_w + hd * RET_DV:2 * qk_w + v_w + (hd + 1) * RET_DV]
            inter_dec = jnp.exp(log_gamma * (pos + 1.0))
            k_dec = jnp.exp(log_gamma * (c - 1.0 - pos))
            chunk_dec = math.exp(log_gamma * c)
            qb = q.astype(BF16)
            scores = _dot_nt(qb, k.astype(BF16)) * dmask_scr[hd]
            s_old = state[hd]
            o = _dot(scores.astype(BF16), v) + _dot(qb, s_old.astype(BF16)) * inter_dec
            state[hd] = s_old * chunk_dec + _dot_tn((k * k_dec).astype(BF16), v)
            y = _silu(gt) * _rms(o)
            y_scr[rows, hd * RET_DV:(hd + 1) * RET_DV] = y.astype(BF16)
    o_ref[0] = h_ref[0] + _dot(y_scr[...], wout_ref[...])


def _ret_layer(h, g, w_in, w_out, cos, sin, c=RET_CHUNK, nchunk=RET_CHUNKS_PER_STEP):
    b_, s_, d = h.shape
    n_in = w_in.shape[1]
    v_w = w_out.shape[0]
    tt = c * nchunk
    z = _norm_matmul(h.reshape(b_ * s_, d), g, w_in.astype(BF16), IN_PROJ_ROWS).reshape(b_, s_, n_in)
    return pl.pallas_call(
        functools.partial(_ret_kernel, c=c, nchunk=nchunk),
        grid=(b_, s_ // tt),
        in_specs=[pl.BlockSpec((1, tt, n_in), lambda bi, ti: (bi, ti, 0)),
                  pl.BlockSpec((1, tt, d), lambda bi, ti: (bi, ti, 0)),
                  pl.BlockSpec((tt, RET_DK // 2), lambda bi, ti: (ti, 0)),
                  pl.BlockSpec((tt, RET_DK // 2), lambda bi, ti: (ti, 0)),
                  pl.BlockSpec((v_w, d), lambda bi, ti: (0, 0), pipeline_mode=pl.Buffered(1))],
        out_specs=pl.BlockSpec((1, tt, d), lambda bi, ti: (bi, ti, 0)),
        out_shape=jax.ShapeDtypeStruct((b_, s_, d), F32),
        scratch_shapes=[pltpu.VMEM((RET_HEADS, RET_DK, RET_DV), F32),
                        pltpu.VMEM((tt, v_w), BF16),
                        pltpu.VMEM((RET_HEADS, c, c), F32)],
        compiler_params=_params(2),
        name="retention_mixer",
    )(z, h, cos, sin, w_out.astype(BF16))


GDN_CHUNKS_PER_STEP = 2
GDN_SIDE_UNITS = 3


def _interleave(main, side, per_step):
    for _ in main:
        for _ in range(per_step):
            next(side, None)
    for _ in side:
        pass


def _gdn_kernel(z_ref, h_ref, cw_ref, alog_ref, dtb_ref, nw_ref, wout_ref, o_ref,
                tail, qkv, qn, kn, state, y_scr, *, c, nchunk):
    nq, nv, dh = GDN_QK_HEADS, GDN_V_HEADS, GDN_D
    qk_w = nq * dh
    conv_ch = 2 * qk_w + nv * dh
    rep = nv // nq
    n = c // SUBLANES

    @pl.when(pl.program_id(1) == 0)
    def _():
        tail[...] = jnp.zeros_like(tail)
        state[...] = jnp.zeros_like(state)

    ri = lax.broadcasted_iota(jnp.int32, (c, c), 0)
    ci = lax.broadcasted_iota(jnp.int32, (c, c), 1)
    eye = ri == ci
    strict = ri > ci
    same = lambda sh: (ri >> sh) == (ci >> sh)
    levels = []
    sh = 4
    while (1 << sh) < c:
        levels.append(same(sh + 1) & jnp.logical_not(same(sh)) & strict)
        sh += 1
    diag_blocks = same(4) & strict
    plus_eye = lambda p: jnp.where(eye, 1.0, p).astype(BF16)
    heads = range(nv)

    def prep(q):
        ctx = dict(prods=[], x0b=[], f0b=[], attnb=[], rhsb=[], q_dec=[], k_decb=[], g_tot=[],
                   lowb=[[] for _ in levels])
        for k in range(conv_ch // LANES):
            xs = [z_ref[q, k, _slab_group(j, n), :] for j in range(n)]
            for j, xc in enumerate(_conv_steps(xs, cw_ref, tail, slice(k * LANES, (k + 1) * LANES))):
                qkv[q, k, _slab_group(j, n), :] = _silu(xc)
            yield
        zs = _slab_get(z_ref.at[q, (conv_ch + nv * dh) // LANES], n)
        beta_all = _sigmoid(zs)
        gcum = _cumsum_rows(-jnp.exp(alog_ref[...]) * _softplus(zs + dtb_ref[...]))
        gcum_t = gcum.T
        yield
        for j in range(nq):
            qj = _slab_get(qkv.at[q, j], n)
            kj = _slab_get(qkv.at[q, nq + j], n)
            qn[q, j] = qj * lax.rsqrt(jnp.sum(qj * qj, axis=-1, keepdims=True) + L2_EPS) * (dh ** -0.5)
            kn[q, j] = kj * lax.rsqrt(jnp.sum(kj * kj, axis=-1, keepdims=True) + L2_EPS)
            ctx["prods"].append(_dot_nt(jnp.concatenate([kn[q, j], qn[q, j]], axis=0).astype(BF16),
                                        kn[q, j].astype(BF16)))
            yield
        for i in heads:
            j = i // rep
            beta = beta_all[:, i:i + 1]
            g_col = gcum[:, nv + i:nv + i + 1]
            g_row = gcum_t[nv + i:nv + i + 1, :]
            g_end = g_col[c - 1:c, :]
            decay = jnp.where(ri >= ci, jnp.exp(jnp.minimum(g_col - g_row, 0.0)), 0.0)
            a_mat = beta * ctx["prods"][j][:c] * decay
            x0 = jnp.where(diag_blocks, -a_mat, 0.0)
            ctx["x0b"].append(x0.astype(BF16))
            ctx["f0b"].append(jnp.where(eye, 1.0, x0).astype(BF16))
            for lvl, mask in enumerate(levels):
                ctx["lowb"][lvl].append(jnp.where(mask, a_mat, 0.0).astype(BF16))
            ctx["attnb"].append((ctx["prods"][j][c:] * decay).astype(BF16))
            exp_g = jnp.exp(g_col)
            v = _slab_get(qkv.at[q, 2 * nq + i], n)
            ctx["rhsb"].append(jnp.concatenate([v * beta, kn[q, j] * (beta * exp_g)], axis=1).astype(BF16))
            ctx["q_dec"].append(qn[q, j] * exp_g)
            ctx["k_decb"].append((kn[q, j] * jnp.exp(g_end - g_col)).astype(BF16))
            ctx["g_tot"].append(jnp.exp(g_end))
            yield
        ctxs[q] = ctx

    def mxu(q):
        cx = ctxs[q]
        p1 = [_dot(x, x) for x in cx["x0b"]]
        yield
        m1 = [_dot(f0, plus_eye(p)).astype(BF16) for f0, p in zip(cx["f0b"], p1)]
        p1b = [p.astype(BF16) for p in p1]
        yield
        p2 = [_dot(p, p) for p in p1b]
        yield
        p2b = [p.astype(BF16) for p in p2]
        p3 = [_dot(pb, pb) for pb in p2b]
        yield
        m2 = [_dot(plus_eye(a), plus_eye(b_)).astype(BF16) for a, b_ in zip(p2, p3)]
        yield
        tb = [_dot(u, w).astype(BF16) for u, w in zip(m1, m2)]
        yield
        for lows in cx["lowb"]:
            corr = [jnp.where(eye, 1.0, -_dot(low, t)).astype(BF16) for low, t in zip(lows, tb)]
            yield
            tb = [_dot(t, g).astype(BF16) for t, g in zip(tb, corr)]
            yield
        uw = [_dot(t, r) for t, r in zip(tb, cx["rhsb"])]
        yield
        s_old = [state[i] for i in heads]
        wq = [_dot(jnp.concatenate([uw[i][:, dh:], cx["q_dec"][i]], axis=0).astype(BF16),
                   s_old[i].astype(BF16)) for i in heads]
        yield
        v_newb = [(uw[i][:, :dh] - wq[i][:c]).astype(BF16) for i in heads]
        cx["o"] = [wq[i][c:] + _dot(cx["attnb"][i], v_newb[i]) for i in heads]
        yield
        for i in heads:
            state[i] = s_old[i] * cx["g_tot"][i] + _dot_tn(cx["k_decb"][i], v_newb[i])
        yield

    def post(q):
        nw = nw_ref[...]
        for i in heads:
            zg = _slab_get(z_ref.at[q, conv_ch // LANES + i], n)
            y_scr[q * c:(q + 1) * c, i * dh:(i + 1) * dh] = (_rms(ctxs[q]["o"][i]) * nw * _silu(zg)).astype(BF16)
            yield

    ctxs = {}
    for _ in prep(0):
        pass
    for q in range(nchunk):
        side = itertools.chain(post(q - 1) if q > 0 else (), prep(q + 1) if q + 1 < nchunk else ())
        _interleave(mxu(q), side, GDN_SIDE_UNITS)
    for _ in post(nchunk - 1):
        pass
    o_ref[0] = h_ref[0] + _dot(y_scr[...], wout_ref[...])


def _gdn_layer(h, g, w_in, conv_w, a_log, dt_bias, norm_w, w_out, c=GDN_CHUNK, nchunk=GDN_CHUNKS_PER_STEP):
    b_, s_, d = h.shape
    v_w = w_out.shape[0]
    conv_ch = conv_w.shape[1]
    main_w = conv_ch + v_w
    small = w_in.shape[1] - main_w
    w_tail = jnp.pad(w_in[:, main_w:], ((0, 0), (0, LANES - small))).astype(BF16)
    n_cat = main_w + LANES
    z = _norm_matmul(h.reshape(b_ * s_, d), g, w_in.astype(BF16), IN_PROJ_ROWS, slab_rows=c,
                     w_tail=w_tail, n=main_w)
    tt = c * nchunk
    nt = s_ // tt
    pad_row = lambda v: jnp.zeros((1, LANES), F32).at[0, GDN_V_HEADS:2 * GDN_V_HEADS].set(v)
    const2 = lambda bi, ti: (0, 0)
    rows = _slab_rows(c // SUBLANES)
    return pl.pallas_call(
        functools.partial(_gdn_kernel, c=c, nchunk=nchunk),
        grid=(b_, nt),
        in_specs=[pl.BlockSpec((nchunk, n_cat // LANES, rows, LANES), lambda bi, ti: (bi * nt + ti, 0, 0, 0)),
                  pl.BlockSpec((1, tt, d), lambda bi, ti: (bi, ti, 0)),
                  pl.BlockSpec((CONV_WIDTH, conv_ch), const2),
                  pl.BlockSpec((1, LANES), const2),
                  pl.BlockSpec((1, LANES), const2),
                  pl.BlockSpec((1, GDN_D), const2),
                  pl.BlockSpec((v_w, d), const2)],
        out_specs=pl.BlockSpec((1, tt, d), lambda bi, ti: (bi, ti, 0)),
        out_shape=jax.ShapeDtypeStruct((b_, s_, d), F32),
        scratch_shapes=[pltpu.VMEM((CONV_WIDTH - 1, SUBLANES, conv_ch), F32),
                        pltpu.VMEM((nchunk, conv_ch // LANES, rows, LANES), F32),
                        pltpu.VMEM((nchunk, GDN_QK_HEADS, c, GDN_D), F32),
                        pltpu.VMEM((nchunk, GDN_QK_HEADS, c, GDN_D), F32),
                        pltpu.VMEM((GDN_V_HEADS, GDN_D, GDN_D), F32),
                        pltpu.VMEM((tt, v_w), BF16)],
        compiler_params=_params(2),
        name="gdn_mixer",
    )(z, h, conv_w, pad_row(a_log), pad_row(dt_bias), norm_w.reshape(1, GDN_D),
      w_out.astype(BF16))


def kernel(x, mix_norm, mlp_norm, mlp_w_up, mlp_w_down, lru_w_in, lru_conv_w, lru_conv_b, lru_wa, lru_ba, lru_wx, lru_bx, lru_lambda, lru_w_out, ret_w_in, ret_w_out, gdn_w_in, gdn_conv_w, gdn_a_log, gdn_dt_bias, gdn_norm, gdn_w_out, final_norm):
    b_, s_, d = x.shape
    depth = mix_norm.shape[0]
    pos = jnp.arange(s_, dtype=F32)
    inv_freq = ROPE_BASE ** (-jnp.arange(0, RET_DK, 2, dtype=F32) / RET_DK)
    ang = pos[:, None] * inv_freq[None, :]
    cos, sin = jnp.cos(ang), jnp.sin(ang)
    lru_w = [w.astype(BF16) for w in (lru_w_in, lru_wa, lru_wx, lru_w_out)]
    h = x
    for i in range(depth):
        kind, j = i % 3, i // 3
        if kind == 0:
            h = _lru_layer(h, mix_norm[i], j, *lru_w, lru_conv_w[j], lru_conv_b[j],
                           lru_ba[j].reshape(-1), lru_bx[j].reshape(-1), lru_lambda[j])
        elif kind == 1:
            h = _ret_layer(h, mix_norm[i], ret_w_in[j], ret_w_out[j], cos, sin)
        else:
            h = _gdn_layer(h, mix_norm[i], gdn_w_in[j], gdn_conv_w[j], gdn_a_log[j],
                           gdn_dt_bias[j], gdn_norm[j], gdn_w_out[j])
        h = _mlp(h.reshape(b_ * s_, d), mlp_norm[i], mlp_w_up, mlp_w_down, i, final_norm, i == depth - 1,
                 MLP_ROWS, MLP_FF_BLOCK).reshape(b_, s_, d)
    return h
```

```python
import functools
import itertools
import math

import jax
import jax.numpy as jnp
from jax import lax
from jax.experimental import pallas as pl
from jax.experimental.pallas import tpu as pltpu

F32 = jnp.float32
BF16 = jnp.bfloat16

RMS_EPS = 1e-6
L2_EPS = 1e-6
CONV_WIDTH = 4
LRU_C = 8.0
LRU_BLOCK = 256
RET_HEADS = 4
RET_DK = 256
RET_DV = 512
ROPE_BASE = 10000.0
GDN_QK_HEADS = 8
GDN_V_HEADS = 16
GDN_D = 128
LANES = 128
SUBLANES = 8
VMEM_LIMIT = 56 * 1024 * 1024

IN_PROJ_ROWS = 512
LRU_ROWS = 512
RET_CHUNK = 256
RET_CHUNKS_PER_STEP = 2
GDN_CHUNK = 128
MLP_ROWS = 2048
MLP_FF_BLOCK = 512

NT_DIMS = (((1,), (1,)), ((), ()))
TN_DIMS = (((0,), (0,)), ((), ()))


def _dot(a, b):
    return jnp.dot(a, b, preferred_element_type=F32)


def _dot_nt(a, b):
    return lax.dot_general(a, b, NT_DIMS, preferred_element_type=F32)


def _dot_tn(a, b):
    return lax.dot_general(a, b, TN_DIMS, preferred_element_type=F32)


def _rms(x):
    return x * lax.rsqrt(jnp.mean(x * x, axis=-1, keepdims=True) + RMS_EPS)


def _sigmoid(x):
    return 0.5 * jnp.tanh(0.5 * x) + 0.5


def _silu(x):
    half = 0.5 * x
    return half * jnp.tanh(half) + half


def _softplus(x):
    return jnp.maximum(x, 0.0) + jnp.log1p(jnp.exp(-jnp.abs(x)))


def _params(n_axes):
    return pltpu.CompilerParams(dimension_semantics=("arbitrary",) * n_axes,
                                vmem_limit_bytes=VMEM_LIMIT)


def _slab_rows(n):
    return SUBLANES * (n + 1)


def _slab_put(slab, x, n):
    for s in range(SUBLANES):
        slab[pl.ds(s * (n + 1), n), :] = x[s * n:(s + 1) * n, :]


def _slab_get(slab, n):
    return jnp.concatenate([slab[pl.ds(s * (n + 1), n), :] for s in range(SUBLANES)], axis=0)


def _slab_group(j, n):
    return pl.ds(j, SUBLANES, stride=n + 1)


def _norm_matmul_kernel(h_ref, g_ref, w_ref, *rest, slab_rows):
    o_ref = rest[-1]
    u = (_rms(h_ref[...]) * g_ref[...]).astype(BF16)
    z = _dot(u, w_ref[...])
    if slab_rows is None:
        o_ref[...] = z
    else:
        n = slab_rows // SUBLANES
        z2 = _dot(u, rest[0][...])
        nk = z.shape[1] // LANES
        for r in range(z.shape[0] // slab_rows):
            rows = slice(r * slab_rows, (r + 1) * slab_rows)
            for k in range(nk + 1):
                piece = z[rows, k * LANES:(k + 1) * LANES] if k < nk else z2[rows, :]
                _slab_put(o_ref.at[r, k], piece, n)
                o_ref[r, k, _slab_group(n, n), :] = jnp.zeros((SUBLANES, LANES), F32)


def _norm_matmul(h2, g, w, tm, slab_rows=None, w_tail=None, n=None):
    t, d = h2.shape
    n = w.shape[1] if n is None else n
    in_specs = [pl.BlockSpec((tm, d), lambda i: (i, 0)),
                pl.BlockSpec((1, d), lambda i: (0, 0)),
                pl.BlockSpec((d, n), lambda i: (0, 0), pipeline_mode=pl.Buffered(1))]
    args = [h2, g.reshape(1, d), w]
    if slab_rows is None:
        out_spec = pl.BlockSpec((tm, n), lambda i: (i, 0))
        out_shape = jax.ShapeDtypeStruct((t, n), F32)
    else:
        rows = _slab_rows(slab_rows // SUBLANES)
        nslab = n // LANES + 1
        in_specs.append(pl.BlockSpec((d, LANES), lambda i: (0, 0), pipeline_mode=pl.Buffered(1)))
        args.append(w_tail)
        out_spec = pl.BlockSpec((tm // slab_rows, nslab, rows, LANES), lambda i: (i, 0, 0, 0))
        out_shape = jax.ShapeDtypeStruct((t // slab_rows, nslab, rows, LANES), F32)
    return pl.pallas_call(
        functools.partial(_norm_matmul_kernel, slab_rows=slab_rows),
        grid=(t // tm,),
        in_specs=in_specs,
        out_specs=out_spec,
        out_shape=out_shape,
        compiler_params=_params(1),
        name="norm_matmul",
    )(*args)


def _mlp_kernel(h_ref, g_ref, wu_ref, wd_ref, fg_ref, o_ref, u_scr, *, final):
    j = pl.program_id(1)

    @pl.when(j == 0)
    def _():
        h = h_ref[...]
        u_scr[...] = (_rms(h) * g_ref[...]).astype(BF16)
        o_ref[...] = h

    hid = _dot(u_scr[...], wu_ref[...].astype(BF16))
    hid = jnp.square(jnp.maximum(hid, 0.0)).astype(BF16)
    o_ref[...] += _dot(hid, wd_ref[...].astype(BF16))

    if final:
        @pl.when(j == pl.num_programs(1) - 1)
        def _():
            o_ref[...] = _rms(o_ref[...]) * fg_ref[...]


def _mlp(h2, g, wu, wd, layer, fg, final, tm, tf):
    t, d = h2.shape
    ff = wu.shape[2]
    return pl.pallas_call(
        functools.partial(_mlp_kernel, final=final),
        grid=(t // tm, ff // tf),
        in_specs=[pl.BlockSpec((tm, d), lambda i, j: (i, 0)),
                  pl.BlockSpec((1, d), lambda i, j: (0, 0)),
                  pl.BlockSpec((None, d, tf), lambda i, j: (layer, 0, j)),
                  pl.BlockSpec((None, tf, d), lambda i, j: (layer, j, 0)),
                  pl.BlockSpec((1, d), lambda i, j: (0, 0))],
        out_specs=pl.BlockSpec((tm, d), lambda i, j: (i, 0)),
        out_shape=jax.ShapeDtypeStruct((t, d), F32),
        scratch_shapes=[pltpu.VMEM((tm, d), BF16)],
        compiler_params=_params(2),
        name="mlp",
    )(h2, g.reshape(1, d), wu, wd, fg.reshape(1, d))


def _conv_steps(xs, cw_ref, tail, lanes):
    n = len(xs)
    sub = lax.broadcasted_iota(jnp.int32, xs[0].shape, 0)
    hist = [pltpu.roll(jnp.where(sub == SUBLANES - 1, tail[k - 1, :, lanes], xs[n - k]), 1, 0)
            for k in range(1, CONV_WIDTH)]
    taps = [jnp.broadcast_to(cw_ref[k:k + 1, lanes], xs[0].shape) for k in range(CONV_WIDTH)]
    at = lambda j: xs[j] if j >= 0 else hist[-j - 1]
    out = []
    for j in range(n):
        acc = taps[CONV_WIDTH - 1] * xs[j]
        for k in range(1, CONV_WIDTH):
            acc = acc + taps[CONV_WIDTH - 1 - k] * at(j - k)
        out.append(acc)
    for k in range(1, CONV_WIDTH):
        tail[k - 1, :, lanes] = xs[n - k]
    return out


def _cumsum_rows(x):
    n = x.shape[0]
    row = lax.broadcasted_iota(jnp.int32, x.shape, 0)
    d = 1
    while d < n:
        x = x + jnp.where(row >= d, pltpu.roll(x, d, 0), 0.0)
        d *= 2
    return x


def _lru_kernel(h_ref, g_ref, win_ref, cw_ref, cb_ref, wa_ref, ba_ref, wx_ref, bx_ref, lam_ref,
                wout_ref, o_ref, tail, carry, x_slab, hs_slab, h0_scr, pp_scr, *, tt, width):
    n = tt // SUBLANES
    bw = LRU_BLOCK
    nblk = width // bw
    cpb = bw // LANES
    grp = lambda t, j: t[j * SUBLANES:(j + 1) * SUBLANES]
    grp_ds = lambda j: pl.ds(j * SUBLANES, SUBLANES)

    @pl.when(pl.program_id(1) == 0)
    def _():
        tail[...] = jnp.zeros_like(tail)
        carry[...] = jnp.zeros_like(carry)

    hn = h_ref[0]
    u = (_rms(hn) * g_ref[...]).astype(BF16)
    gate, out = {}, [hn]

    def x_proj(blk):
        zx = _dot(u, win_ref[:, width + blk * bw:width + (blk + 1) * bw])
        for c in range(cpb):
            _slab_put(x_slab.at[blk * cpb + c], zx[:, c * LANES:(c + 1) * LANES], n)

    def gate_proj(blk):
        gate[blk] = _dot(u, win_ref[:, blk * bw:(blk + 1) * bw])

    def recur(blk):
        lanes = slice(blk * bw, (blk + 1) * bw)
        xg = [jnp.concatenate(g, axis=1) for g in zip(*[
            _conv_steps([x_slab[blk * cpb + c, _slab_group(j, n), :] for j in range(n)],
                        cw_ref, tail, slice(blk * bw + c * LANES, blk * bw + (c + 1) * LANES))
            for c in range(cpb)])]
        cb = cb_ref[:, lanes]
        xg = [g + cb for g in xg]
        xb = jnp.concatenate(xg, axis=0).astype(BF16)
        ra = _dot(xb, wa_ref[blk])
        ia = _dot(xb, wx_ref[blk])
        ba = ba_ref[:, lanes]
        bx = bx_ref[:, lanes]
        sp = -LRU_C * _softplus(-lam_ref[:, lanes])
        hc = jnp.zeros((SUBLANES, bw), F32)
        pc = jnp.ones((SUBLANES, bw), F32)
        for j in range(n):
            r = _sigmoid(grp(ra, j) + ba)
            i = _sigmoid(grp(ia, j) + bx)
            aj = jnp.exp(r * sp)
            om = 1.0 - aj * aj
            bj = om * lax.rsqrt(jnp.maximum(om, 1e-30)) * (i * xg[j])
            hc = aj * hc + bj
            pc = aj * pc
            h0_scr[grp_ds(j), lanes] = hc
            pp_scr[grp_ds(j), lanes] = pc
        sub = lax.broadcasted_iota(jnp.int32, (SUBLANES, bw), 0)
        ea, eb = pc, hc
        d = 1
        while d < SUBLANES:
            keep = sub >= d
            eb = jnp.where(keep, ea * pltpu.roll(eb, d, 0) + eb, eb)
            ea = jnp.where(keep, ea * pltpu.roll(ea, d, 0), ea)
            d *= 2
        c0 = carry[:, lanes]
        seg_end = eb + ea * c0
        c_in = jnp.where(sub == 0, c0, pltpu.roll(seg_end, 1, 0))
        carry[:, lanes] = jnp.broadcast_to(seg_end[SUBLANES - 1:SUBLANES, :], (SUBLANES, bw))
        for j in range(n):
            hs = h0_scr[grp_ds(j), lanes] + pp_scr[grp_ds(j), lanes] * c_in
            for c in range(cpb):
                hs_slab[blk * cpb + c, _slab_group(j, n), :] = hs[:, c * LANES:(c + 1) * LANES]

    def out_piece(blk):
        hs_time = jnp.concatenate([_slab_get(hs_slab.at[blk * cpb + c], n) for c in range(cpb)], axis=1)
        y = (jax.nn.gelu(gate[blk]) * hs_time).astype(BF16)
        out[0] = out[0] + _dot(y, wout_ref[blk * bw:(blk + 1) * bw, :])

    x_proj(0)
    for blk in range(nblk):
        if blk + 1 < nblk:
            x_proj(blk + 1)
        gate_proj(blk)
        recur(blk)
        if blk > 0:
            out_piece(blk - 1)
    out_piece(nblk - 1)
    o_ref[0] = out[0]


def _lru_layer(h, g, layer, w_in, wa, wx, w_out, conv_w, conv_b, ba, bx, lam, tt=LRU_ROWS):
    b_, s_, d = h.shape
    width = w_out.shape[1]
    row = lambda v: v.reshape(1, width)
    const2 = lambda bi, ti: (0, 0)
    slab = pltpu.VMEM((width // LANES, _slab_rows(tt // SUBLANES), LANES), F32)
    return pl.pallas_call(
        functools.partial(_lru_kernel, tt=tt, width=width),
        grid=(b_, s_ // tt),
        in_specs=[pl.BlockSpec((1, tt, d), lambda bi, ti: (bi, ti, 0)),
                  pl.BlockSpec((1, d), const2),
                  pl.BlockSpec((None, d, 2 * width), lambda bi, ti: (layer, 0, 0)),
                  pl.BlockSpec((CONV_WIDTH, width), const2),
                  pl.BlockSpec((1, width), const2),
                  pl.BlockSpec((None,) + wa.shape[1:], lambda bi, ti: (layer, 0, 0, 0)),
                  pl.BlockSpec((1, width), const2),
                  pl.BlockSpec((None,) + wx.shape[1:], lambda bi, ti: (layer, 0, 0, 0)),
                  pl.BlockSpec((1, width), const2),
                  pl.BlockSpec((1, width), const2),
                  pl.BlockSpec((None, width, d), lambda bi, ti: (layer, 0, 0))],
        out_specs=pl.BlockSpec((1, tt, d), lambda bi, ti: (bi, ti, 0)),
        out_shape=jax.ShapeDtypeStruct((b_, s_, d), F32),
        scratch_shapes=[pltpu.VMEM((CONV_WIDTH - 1, SUBLANES, width), F32),
                        pltpu.VMEM((SUBLANES, width), F32), slab, slab,
                        pltpu.VMEM((tt, width), F32), pltpu.VMEM((tt, width), F32)],
        compiler_params=_params(2),
        name="lru_mixer",
    )(h, g.reshape(1, d), w_in, conv_w, row(conv_b), wa, row(ba), wx, row(bx), row(lam), w_out)


def _rope(x, cos, sin):
    half = x.shape[1] // 2
    x1, x2 = x[:, :half], x[:, half:]
    return jnp.concatenate([x1 * cos - x2 * sin, x2 * cos + x1 * sin], axis=1)


def _ret_kernel(z_ref, h_ref, cos_ref, sin_ref, wout_ref, o_ref, state, y_scr, dmask_scr, *, c, nchunk):
    log_gammas = [math.log1p(-(2.0 ** (-5.0 - hd))) for hd in range(RET_HEADS)]

    @pl.when(pl.program_id(1) == 0)
    def _():
        state[...] = jnp.zeros_like(state)
        ri = lax.broadcasted_iota(jnp.int32, (c, c), 0)
        ci = lax.broadcasted_iota(jnp.int32, (c, c), 1)
        diff = (ri - ci).astype(F32)
        for hd in range(RET_HEADS):
            dmask_scr[hd] = jnp.where(diff >= 0.0, jnp.exp(log_gammas[hd] * jnp.maximum(diff, 0.0)), 0.0)

    pos = lax.broadcasted_iota(jnp.int32, (c, 1), 0).astype(F32)
    qk_w = RET_HEADS * RET_DK
    v_w = RET_HEADS * RET_DV
    for q_ in range(nchunk):
        rows = slice(q_ * c, (q_ + 1) * c)
        cos = cos_ref[rows, :]
        sin = sin_ref[rows, :]
        for hd in range(RET_HEADS):
            log_gamma = log_gammas[hd]
            q = _rope(z_ref[0, rows, hd * RET_DK:(hd + 1) * RET_DK], cos, sin)
            k = _rope(z_ref[0, rows, qk_w + hd * RET_DK:qk_w + (hd + 1) * RET_DK], cos, sin) * (RET_DK ** -0.5)
            v = z_ref[0, rows, 2 * qk_w + hd * RET_DV:2 * qk_w + (hd + 1) * RET_DV].astype(BF16)
            gt = z_ref[0, rows, 2 * qk_w + v_w + hd * RET_DV:2 * qk_w + v_w + (hd + 1) * RET_DV]
            inter_dec = jnp.exp(log_gamma * (pos + 1.0))
            k_dec = jnp.exp(log_gamma * (c - 1.0 - pos))
            chunk_dec = math.exp(log_gamma * c)
            qb = q.astype(BF16)
            scores = _dot_nt(qb, k.astype(BF16)) * dmask_scr[hd]
            s_old = state[hd]
            o = _dot(scores.astype(BF16), v) + _dot(qb, s_old.astype(BF16)) * inter_dec
            state[hd] = s_old * chunk_dec + _dot_tn((k * k_dec).astype(BF16), v)
            y = _silu(gt) * _rms(o)
            y_scr[rows, hd * RET_DV:(hd + 1) * RET_DV] = y.astype(BF16)
    o_ref[0] = h_ref[0] + _dot(y_scr[...], wout_ref[...])


def _ret_layer(h, g, w_in, w_out, cos, sin, c=RET_CHUNK, nchunk=RET_CHUNKS_PER_STEP):
    b_, s_, d = h.shape
    n_in = w_in.shape[1]
    v_w = w_out.shape[0]
    tt = c * nchunk
    z = _norm_matmul(h.reshape(b_ * s_, d), g, w_in.astype(BF16), IN_PROJ_ROWS).reshape(b_, s_, n_in)
    return pl.pallas_call(
        functools.partial(_ret_kernel, c=c, nchunk=nchunk),
        grid=(b_, s_ // tt),
        in_specs=[pl.BlockSpec((1, tt, n_in), lambda bi, ti: (bi, ti, 0)),
                  pl.BlockSpec((1, tt, d), lambda bi, ti: (bi, ti, 0)),
                  pl.BlockSpec((tt, RET_DK // 2), lambda bi, ti: (ti, 0)),
                  pl.BlockSpec((tt, RET_DK // 2), lambda bi, ti: (ti, 0)),
                  pl.BlockSpec((v_w, d), lambda bi, ti: (0, 0), pipeline_mode=pl.Buffered(1))],
        out_specs=pl.BlockSpec((1, tt, d), lambda bi, ti: (bi, ti, 0)),
        out_shape=jax.ShapeDtypeStruct((b_, s_, d), F32),
        scratch_shapes=[pltpu.VMEM((RET_HEADS, RET_DK, RET_DV), F32),
                        pltpu.VMEM((tt, v_w), BF16),
                        pltpu.VMEM((RET_HEADS, c, c), F32)],
        compiler_params=_params(2),
        name="retention_mixer",
    )(z, h, cos, sin, w_out.astype(BF16))


GDN_CHUNKS_PER_STEP = 2
GDN_SIDE_UNITS = 3


def _interleave(main, side, per_step):
    for _ in main:
        for _ in range(per_step):
            next(side, None)
    for _ in side:
        pass


def _gdn_kernel(z_ref, h_ref, cw_ref, alog_ref, dtb_ref, nw_ref, wout_ref, o_ref,
                tail, qkv, qn, kn, state, y_scr, *, c, nchunk):
    nq, nv, dh = GDN_QK_HEADS, GDN_V_HEADS, GDN_D
    qk_w = nq * dh
    conv_ch = 2 * qk_w + nv * dh
    rep = nv // nq
    n = c // SUBLANES

    @pl.when(pl.program_id(1) == 0)
    def _():
        tail[...] = jnp.zeros_like(tail)
        state[...] = jnp.zeros_like(state)

    ri = lax.broadcasted_iota(jnp.int32, (c, c), 0)
    ci = lax.broadcasted_iota(jnp.int32, (c, c), 1)
    eye = ri == ci
    strict = ri > ci
    same = lambda sh: (ri >> sh) == (ci >> sh)
    levels = []
    sh = 4
    while (1 << sh) < c:
        levels.append(same(sh + 1) & jnp.logical_not(same(sh)) & strict)
        sh += 1
    diag_blocks = same(4) & strict
    plus_eye = lambda p: jnp.where(eye, 1.0, p).astype(BF16)
    heads = range(nv)

    def prep(q):
        ctx = dict(prods=[], x0b=[], f0b=[], attnb=[], rhsb=[], q_dec=[], k_decb=[], g_tot=[],
                   lowb=[[] for _ in levels])
        for k in range(conv_ch // LANES):
            xs = [z_ref[q, k, _slab_group(j, n), :] for j in range(n)]
            for j, xc in enumerate(_conv_steps(xs, cw_ref, tail, slice(k * LANES, (k + 1) * LANES))):
                qkv[q, k, _slab_group(j, n), :] = _silu(xc)
            yield
        zs = _slab_get(z_ref.at[q, (conv_ch + nv * dh) // LANES], n)
        beta_all = _sigmoid(zs)
        gcum = _cumsum_rows(-jnp.exp(alog_ref[...]) * _softplus(zs + dtb_ref[...]))
        gcum_t = gcum.T
        yield
        for j in range(nq):
            qj = _slab_get(qkv.at[q, j], n)
            kj = _slab_get(qkv.at[q, nq + j], n)
            qn[q, j] = qj * lax.rsqrt(jnp.sum(qj * qj, axis=-1, keepdims=True) + L2_EPS) * (dh ** -0.5)
            kn[q, j] = kj * lax.rsqrt(jnp.sum(kj * kj, axis=-1, keepdims=True) + L2_EPS)
            ctx["prods"].append(_dot_nt(jnp.concatenate([kn[q, j], qn[q, j]], axis=0).astype(BF16),
                                        kn[q, j].astype(BF16)))
            yield
        for i in heads:
            j = i // rep
            beta = beta_all[:, i:i + 1]
            g_col = gcum[:, nv + i:nv + i + 1]
            g_row = gcum_t[nv + i:nv + i + 1, :]
            g_end = g_col[c - 1:c, :]
            decay = jnp.where(ri >= ci, jnp.exp(jnp.minimum(g_col - g_row, 0.0)), 0.0)
            a_mat = beta * ctx["prods"][j][:c] * decay
            x0 = jnp.where(diag_blocks, -a_mat, 0.0)
            ctx["x0b"].append(x0.astype(BF16))
            ctx["f0b"].append(jnp.where(eye, 1.0, x0).astype(BF16))
            for lvl, mask in enumerate(levels):
                ctx["lowb"][lvl].append(jnp.where(mask, a_mat, 0.0).astype(BF16))
            ctx["attnb"].append((ctx["prods"][j][c:] * decay).astype(BF16))
            exp_g = jnp.exp(g_col)
            v = _slab_get(qkv.at[q, 2 * nq + i], n)
            ctx["rhsb"].append(jnp.concatenate([v * beta, kn[q, j] * (beta * exp_g)], axis=1).astype(BF16))
            ctx["q_dec"].append(qn[q, j] * exp_g)
            ctx["k_decb"].append((kn[q, j] * jnp.exp(g_end - g_col)).astype(BF16))
            ctx["g_tot"].append(jnp.exp(g_end))
            yield
        ctxs[q] = ctx

    def mxu(q):
        cx = ctxs[q]
        p1 = [_dot(x, x) for x in cx["x0b"]]
        yield
        m1 = [_dot(f0, plus_eye(p)).astype(BF16) for f0, p in zip(cx["f0b"], p1)]
        p1b = [p.astype(BF16) for p in p1]
        yield
        p2 = [_dot(p, p) for p in p1b]
        yield
        p2b = [p.astype(BF16) for p in p2]
        p3 = [_dot(pb, pb) for pb in p2b]
        yield
        m2 = [_dot(plus_eye(a), plus_eye(b_)).astype(BF16) for a, b_ in zip(p2, p3)]
        yield
        tb = [_dot(u, w).astype(BF16) for u, w in zip(m1, m2)]
        yield
        for lows in cx["lowb"]:
            corr = [jnp.where(eye, 1.0, -_dot(low, t)).astype(BF16) for low, t in zip(lows, tb)]
            yield
            tb = [_dot(t, g).astype(BF16) for t, g in zip(tb, corr)]
            yield
        uw = [_dot(t, r) for t, r in zip(tb, cx["rhsb"])]
        yield
        s_old = [state[i] for i in heads]
        wq = [_dot(jnp.concatenate([uw[i][:, dh:], cx["q_dec"][i]], axis=0).astype(BF16),
                   s_old[i].astype(BF16)) for i in heads]
        yield
        v_newb = [(uw[i][:, :dh] - wq[i][:c]).astype(BF16) for i in heads]
        cx["o"] = [wq[i][c:] + _dot(cx["attnb"][i], v_newb[i]) for i in heads]
        yield
        for i in heads:
            state[i] = s_old[i] * cx["g_tot"][i] + _dot_tn(cx["k_decb"][i], v_newb[i])
        yield

    def post(q):
        nw = nw_ref[...]
        for i in heads:
            zg = _slab_get(z_ref.at[q, conv_ch // LANES + i], n)
            y_scr[q * c:(q + 1) * c, i * dh:(i + 1) * dh] = (_rms(ctxs[q]["o"][i]) * nw * _silu(zg)).astype(BF16)
            yield

    ctxs = {}
    for _ in prep(0):
        pass
    for q in range(nchunk):
        side = itertools.chain(post(q - 1) if q > 0 else (), prep(q + 1) if q + 1 < nchunk else ())
        _interleave(mxu(q), side, GDN_SIDE_UNITS)
    for _ in post(nchunk - 1):
        pass
    o_ref[0] = h_ref[0] + _dot(y_scr[...], wout_ref[...])


def _gdn_layer(h, g, w_in, conv_w, a_log, dt_bias, norm_w, w_out, c=GDN_CHUNK, nchunk=GDN_CHUNKS_PER_STEP):
    b_, s_, d = h.shape
    v_w = w_out.shape[0]
    conv_ch = conv_w.shape[1]
    main_w = conv_ch + v_w
    small = w_in.shape[1] - main_w
    w_tail = jnp.pad(w_in[:, main_w:], ((0, 0), (0, LANES - small))).astype(BF16)
    n_cat = main_w + LANES
    z = _norm_matmul(h.reshape(b_ * s_, d), g, w_in.astype(BF16), IN_PROJ_ROWS, slab_rows=c,
                     w_tail=w_tail, n=main_w)
    tt = c * nchunk
    nt = s_ // tt
    pad_row = lambda v: jnp.zeros((1, LANES), F32).at[0, GDN_V_HEADS:2 * GDN_V_HEADS].set(v)
    const2 = lambda bi, ti: (0, 0)
    rows = _slab_rows(c // SUBLANES)
    return pl.pallas_call(
        functools.partial(_gdn_kernel, c=c, nchunk=nchunk),
        grid=(b_, nt),
        in_specs=[pl.BlockSpec((nchunk, n_cat // LANES, rows, LANES), lambda bi, ti: (bi * nt + ti, 0, 0, 0)),
                  pl.BlockSpec((1, tt, d), lambda bi, ti: (bi, ti, 0)),
                  pl.BlockSpec((CONV_WIDTH, conv_ch), const2),
                  pl.BlockSpec((1, LANES), const2),
                  pl.BlockSpec((1, LANES), const2),
                  pl.BlockSpec((1, GDN_D), const2),
                  pl.BlockSpec((v_w, d), const2)],
        out_specs=pl.BlockSpec((1, tt, d), lambda bi, ti: (bi, ti, 0)),
        out_shape=jax.ShapeDtypeStruct((b_, s_, d), F32),
        scratch_shapes=[pltpu.VMEM((CONV_WIDTH - 1, SUBLANES, conv_ch), F32),
                        pltpu.VMEM((nchunk, conv_ch // LANES, rows, LANES), F32),
                        pltpu.VMEM((nchunk, GDN_QK_HEADS, c, GDN_D), F32),
                        pltpu.VMEM((nchunk, GDN_QK_HEADS, c, GDN_D), F32),
                        pltpu.VMEM((GDN_V_HEADS, GDN_D, GDN_D), F32),
                        pltpu.VMEM((tt, v_w), BF16)],
        compiler_params=_params(2),
        name="gdn_mixer",
    )(z, h, conv_w, pad_row(a_log), pad_row(dt_bias), norm_w.reshape(1, GDN_D),
      w_out.astype(BF16))


def kernel(x, mix_norm, mlp_norm, mlp_w_up, mlp_w_down, lru_w_in, lru_conv_w, lru_conv_b, lru_wa, lru_ba, lru_wx, lru_bx, lru_lambda, lru_w_out, ret_w_in, ret_w_out, gdn_w_in, gdn_conv_w, gdn_a_log, gdn_dt_bias, gdn_norm, gdn_w_out, final_norm):
    b_, s_, d = x.shape
    depth = mix_norm.shape[0]
    pos = jnp.arange(s_, dtype=F32)
    inv_freq = ROPE_BASE ** (-jnp.arange(0, RET_DK, 2, dtype=F32) / RET_DK)
    ang = pos[:, None] * inv_freq[None, :]
    cos, sin = jnp.cos(ang), jnp.sin(ang)
    lru_w = [w.astype(BF16) for w in (lru_w_in, lru_wa, lru_wx, lru_w_out)]
    h = x
    for i in range(depth):
        kind, j = i % 3, i // 3
        if kind == 0:
            h = _lru_layer(h, mix_norm[i], j, *lru_w, lru_conv_w[j], lru_conv_b[j],
                           lru_ba[j].reshape(-1), lru_bx[j].reshape(-1), lru_lambda[j])
        elif kind == 1:
            h = _ret_layer(h, mix_norm[i], ret_w_in[j], ret_w_out[j], cos, sin)
        else:
            h = _gdn_layer(h, mix_norm[i], gdn_w_in[j], gdn_conv_w[j], gdn_a_log[j],
                           gdn_dt_bias[j], gdn_norm[j], gdn_w_out[j])
        h = _mlp(h.reshape(b_ * s_, d), mlp_norm[i], mlp_w_up, mlp_w_down, i, final_norm, i == depth - 1,
                 MLP_ROWS, MLP_FF_BLOCK).reshape(b_, s_, d)
    return h
```

```python
import functools
import itertools
import math

import jax
import jax.numpy as jnp
from jax import lax
from jax.experimental import pallas as pl
from jax.experimental.pallas import tpu as pltpu

F32 = jnp.float32
BF16 = jnp.bfloat16

RMS_EPS = 1e-6
L2_EPS = 1e-6
CONV_WIDTH = 4
LRU_C = 8.0
LRU_BLOCK = 256
RET_HEADS = 4
RET_DK = 256
RET_DV = 512
ROPE_BASE = 10000.0
GDN_QK_HEADS = 8
GDN_V_HEADS = 16
GDN_D = 128
LANES = 128
SUBLANES = 8
VMEM_LIMIT = 56 * 1024 * 1024

IN_PROJ_ROWS = 512
LRU_ROWS = 512
RET_CHUNK = 256
RET_CHUNKS_PER_STEP = 2
GDN_CHUNK = 128
MLP_ROWS = 1024
MLP_FF_BLOCK = 1024

NT_DIMS = (((1,), (1,)), ((), ()))
TN_DIMS = (((0,), (0,)), ((), ()))


def _dot(a, b):
    return jnp.dot(a, b, preferred_element_type=F32)


def _dot_nt(a, b):
    return lax.dot_general(a, b, NT_DIMS, preferred_element_type=F32)


def _dot_tn(a, b):
    return lax.dot_general(a, b, TN_DIMS, preferred_element_type=F32)


def _rms(x):
    return x * lax.rsqrt(jnp.mean(x * x, axis=-1, keepdims=True) + RMS_EPS)


def _sigmoid(x):
    return 0.5 * jnp.tanh(0.5 * x) + 0.5


def _silu(x):
    half = 0.5 * x
    return half * jnp.tanh(half) + half


def _softplus(x):
    return jnp.maximum(x, 0.0) + jnp.log1p(jnp.exp(-jnp.abs(x)))


def _params(n_axes):
    return pltpu.CompilerParams(dimension_semantics=("arbitrary",) * n_axes,
                                vmem_limit_bytes=VMEM_LIMIT)


def _slab_rows(n):
    return SUBLANES * (n + 1)


def _slab_put(slab, x, n):
    for s in range(SUBLANES):
        slab[pl.ds(s * (n + 1), n), :] = x[s * n:(s + 1) * n, :]


def _slab_get(slab, n):
    return jnp.concatenate([slab[pl.ds(s * (n + 1), n), :] for s in range(SUBLANES)], axis=0)


def _slab_group(j, n):
    return pl.ds(j, SUBLANES, stride=n + 1)


def _norm_matmul_kernel(h_ref, g_ref, w_ref, *rest, slab_rows, bf16_cols=None):
    o_ref = rest[-1]
    u = (_rms(h_ref[...]) * g_ref[...]).astype(BF16)
    z = _dot(u, w_ref[...])
    if bf16_cols is not None:
        lo, hi = bf16_cols
        rest[-2][...] = jnp.concatenate([z[:, :lo], z[:, hi:]], axis=1)
        o_ref[...] = z[:, lo:hi].astype(BF16)
    elif slab_rows is None:
        o_ref[...] = z
    else:
        n = slab_rows // SUBLANES
        z2 = _dot(u, rest[0][...])
        nk = z.shape[1] // LANES
        for r in range(z.shape[0] // slab_rows):
            rows = slice(r * slab_rows, (r + 1) * slab_rows)
            for k in range(nk + 1):
                piece = z[rows, k * LANES:(k + 1) * LANES] if k < nk else z2[rows, :]
                _slab_put(o_ref.at[r, k], piece, n)
                o_ref[r, k, _slab_group(n, n), :] = jnp.zeros((SUBLANES, LANES), F32)


def _norm_matmul(h2, g, w, tm, slab_rows=None, w_tail=None, n=None, bf16_cols=None):
    t, d = h2.shape
    n = w.shape[1] if n is None else n
    in_specs = [pl.BlockSpec((tm, d), lambda i: (i, 0)),
                pl.BlockSpec((1, d), lambda i: (0, 0)),
                pl.BlockSpec((d, n), lambda i: (0, 0), pipeline_mode=pl.Buffered(1))]
    args = [h2, g.reshape(1, d), w]
    if bf16_cols is not None:
        nb = bf16_cols[1] - bf16_cols[0]
        out_spec = (pl.BlockSpec((tm, n - nb), lambda i: (i, 0)), pl.BlockSpec((tm, nb), lambda i: (i, 0)))
        out_shape = (jax.ShapeDtypeStruct((t, n - nb), F32), jax.ShapeDtypeStruct((t, nb), BF16))
    elif slab_rows is None:
        out_spec = pl.BlockSpec((tm, n), lambda i: (i, 0))
        out_shape = jax.ShapeDtypeStruct((t, n), F32)
    else:
        rows = _slab_rows(slab_rows // SUBLANES)
        nslab = n // LANES + 1
        in_specs.append(pl.BlockSpec((d, LANES), lambda i: (0, 0), pipeline_mode=pl.Buffered(1)))
        args.append(w_tail)
        out_spec = pl.BlockSpec((tm // slab_rows, nslab, rows, LANES), lambda i: (i, 0, 0, 0))
        out_shape = jax.ShapeDtypeStruct((t // slab_rows, nslab, rows, LANES), F32)
    return pl.pallas_call(
        functools.partial(_norm_matmul_kernel, slab_rows=slab_rows, bf16_cols=bf16_cols),
        grid=(t // tm,),
        in_specs=in_specs,
        out_specs=out_spec,
        out_shape=out_shape,
        compiler_params=_params(1),
        name="norm_matmul",
    )(*args)


def _mlp_kernel(h_ref, g_ref, wu_ref, wd_ref, fg_ref, o_ref, u_scr, *, final):
    j = pl.program_id(1)

    @pl.when(j == 0)
    def _():
        h = h_ref[...]
        u_scr[...] = (_rms(h) * g_ref[...]).astype(BF16)
        o_ref[...] = h

    hid = _dot(u_scr[...], wu_ref[...].astype(BF16))
    hid = jnp.square(jnp.maximum(hid, 0.0)).astype(BF16)
    o_ref[...] += _dot(hid, wd_ref[...].astype(BF16))

    if final:
        @pl.when(j == pl.num_programs(1) - 1)
        def _():
            o_ref[...] = _rms(o_ref[...]) * fg_ref[...]


def _mlp(h2, g, wu, wd, layer, fg, final, tm, tf):
    t, d = h2.shape
    ff = wu.shape[2]
    return pl.pallas_call(
        functools.partial(_mlp_kernel, final=final),
        grid=(t // tm, ff // tf),
        in_specs=[pl.BlockSpec((tm, d), lambda i, j: (i, 0)),
                  pl.BlockSpec((1, d), lambda i, j: (0, 0)),
                  pl.BlockSpec((None, d, tf), lambda i, j: (layer, 0, j)),
                  pl.BlockSpec((None, tf, d), lambda i, j: (layer, j, 0)),
                  pl.BlockSpec((1, d), lambda i, j: (0, 0))],
        out_specs=pl.BlockSpec((tm, d), lambda i, j: (i, 0)),
        out_shape=jax.ShapeDtypeStruct((t, d), F32),
        scratch_shapes=[pltpu.VMEM((tm, d), BF16)],
        compiler_params=_params(2),
        name="mlp",
    )(h2, g.reshape(1, d), wu, wd, fg.reshape(1, d))


def _conv_steps(xs, cw_ref, tail, lanes):
    n = len(xs)
    sub = lax.broadcasted_iota(jnp.int32, xs[0].shape, 0)
    hist = [pltpu.roll(jnp.where(sub == SUBLANES - 1, tail[k - 1, :, lanes], xs[n - k]), 1, 0)
            for k in range(1, CONV_WIDTH)]
    taps = [jnp.broadcast_to(cw_ref[k:k + 1, lanes], xs[0].shape) for k in range(CONV_WIDTH)]
    at = lambda j: xs[j] if j >= 0 else hist[-j - 1]
    out = []
    for j in range(n):
        acc = taps[CONV_WIDTH - 1] * xs[j]
        for k in range(1, CONV_WIDTH):
            acc = acc + taps[CONV_WIDTH - 1 - k] * at(j - k)
        out.append(acc)
    for k in range(1, CONV_WIDTH):
        tail[k - 1, :, lanes] = xs[n - k]
    return out


def _cumsum_rows(x):
    n = x.shape[0]
    row = lax.broadcasted_iota(jnp.int32, x.shape, 0)
    d = 1
    while d < n:
        x = x + jnp.where(row >= d, pltpu.roll(x, d, 0), 0.0)
        d *= 2
    return x


def _lru_kernel(h_ref, g_ref, win_ref, cw_ref, cb_ref, wa_ref, ba_ref, wx_ref, bx_ref, lam_ref,
                wout_ref, o_ref, tail, carry, x_slab, hs_slab, h0_scr, pp_scr, *, tt, width):
    n = tt // SUBLANES
    bw = LRU_BLOCK
    nblk = width // bw
    cpb = bw // LANES
    grp = lambda t, j: t[j * SUBLANES:(j + 1) * SUBLANES]
    grp_ds = lambda j: pl.ds(j * SUBLANES, SUBLANES)

    @pl.when(pl.program_id(1) == 0)
    def _():
        tail[...] = jnp.zeros_like(tail)
        carry[...] = jnp.zeros_like(carry)

    hn = h_ref[0]
    u = (_rms(hn) * g_ref[...]).astype(BF16)
    gate, out = {}, [hn]

    def x_proj(blk):
        zx = _dot(u, win_ref[:, width + blk * bw:width + (blk + 1) * bw])
        for c in range(cpb):
            _slab_put(x_slab.at[blk * cpb + c], zx[:, c * LANES:(c + 1) * LANES], n)

    def gate_proj(blk):
        gate[blk] = _dot(u, win_ref[:, blk * bw:(blk + 1) * bw])

    def recur(blk):
        lanes = slice(blk * bw, (blk + 1) * bw)
        xg = [jnp.concatenate(g, axis=1) for g in zip(*[
            _conv_steps([x_slab[blk * cpb + c, _slab_group(j, n), :] for j in range(n)],
                        cw_ref, tail, slice(blk * bw + c * LANES, blk * bw + (c + 1) * LANES))
            for c in range(cpb)])]
        cb = cb_ref[:, lanes]
        xg = [g + cb for g in xg]
        xb = jnp.concatenate(xg, axis=0).astype(BF16)
        ra = _dot(xb, wa_ref[blk])
        ia = _dot(xb, wx_ref[blk])
        ba = ba_ref[:, lanes]
        bx = bx_ref[:, lanes]
        sp = -LRU_C * _softplus(-lam_ref[:, lanes])
        hc = jnp.zeros((SUBLANES, bw), F32)
        pc = jnp.ones((SUBLANES, bw), F32)
        for j in range(n):
            r = _sigmoid(grp(ra, j) + ba)
            i = _sigmoid(grp(ia, j) + bx)
            aj = jnp.exp(r * sp)
            om = 1.0 - aj * aj
            bj = om * lax.rsqrt(jnp.maximum(om, 1e-30)) * (i * xg[j])
            hc = aj * hc + bj
            pc = aj * pc
            h0_scr[grp_ds(j), lanes] = hc
            pp_scr[grp_ds(j), lanes] = pc
        sub = lax.broadcasted_iota(jnp.int32, (SUBLANES, bw), 0)
        ea, eb = pc, hc
        d = 1
        while d < SUBLANES:
            keep = sub >= d
            eb = jnp.where(keep, ea * pltpu.roll(eb, d, 0) + eb, eb)
            ea = jnp.where(keep, ea * pltpu.roll(ea, d, 0), ea)
            d *= 2
        c0 = carry[:, lanes]
        seg_end = eb + ea * c0
        c_in = jnp.where(sub == 0, c0, pltpu.roll(seg_end, 1, 0))
        carry[:, lanes] = jnp.broadcast_to(seg_end[SUBLANES - 1:SUBLANES, :], (SUBLANES, bw))
        for j in range(n):
            hs = h0_scr[grp_ds(j), lanes] + pp_scr[grp_ds(j), lanes] * c_in
            for c in range(cpb):
                hs_slab[blk * cpb + c, _slab_group(j, n), :] = hs[:, c * LANES:(c + 1) * LANES]

    def out_piece(blk):
        hs_time = jnp.concatenate([_slab_get(hs_slab.at[blk * cpb + c], n) for c in range(cpb)], axis=1)
        y = (jax.nn.gelu(gate[blk]) * hs_time).astype(BF16)
        out[0] = out[0] + _dot(y, wout_ref[blk * bw:(blk + 1) * bw, :])

    x_proj(0)
    for blk in range(nblk):
        if blk + 1 < nblk:
            x_proj(blk + 1)
        gate_proj(blk)
        recur(blk)
        if blk > 0:
            out_piece(blk - 1)
    out_piece(nblk - 1)
    o_ref[0] = out[0]


def _lru_layer(h, g, layer, w_in, wa, wx, w_out, conv_w, conv_b, ba, bx, lam, tt=LRU_ROWS):
    b_, s_, d = h.shape
    width = w_out.shape[1]
    row = lambda v: v.reshape(1, width)
    const2 = lambda bi, ti: (0, 0)
    slab = pltpu.VMEM((width // LANES, _slab_rows(tt // SUBLANES), LANES), F32)
    return pl.pallas_call(
        functools.partial(_lru_kernel, tt=tt, width=width),
        grid=(b_, s_ // tt),
        in_specs=[pl.BlockSpec((1, tt, d), lambda bi, ti: (bi, ti, 0)),
                  pl.BlockSpec((1, d), const2),
                  pl.BlockSpec((None, d, 2 * width), lambda bi, ti: (layer, 0, 0)),
                  pl.BlockSpec((CONV_WIDTH, width), const2),
                  pl.BlockSpec((1, width), const2),
                  pl.BlockSpec((None,) + wa.shape[1:], lambda bi, ti: (layer, 0, 0, 0)),
                  pl.BlockSpec((1, width), const2),
                  pl.BlockSpec((None,) + wx.shape[1:], lambda bi, ti: (layer, 0, 0, 0)),
                  pl.BlockSpec((1, width), const2),
                  pl.BlockSpec((1, width), const2),
                  pl.BlockSpec((None, width, d), lambda bi, ti: (layer, 0, 0))],
        out_specs=pl.BlockSpec((1, tt, d), lambda bi, ti: (bi, ti, 0)),
        out_shape=jax.ShapeDtypeStruct((b_, s_, d), F32),
        scratch_shapes=[pltpu.VMEM((CONV_WIDTH - 1, SUBLANES, width), F32),
                        pltpu.VMEM((SUBLANES, width), F32), slab, slab,
                        pltpu.VMEM((tt, width), F32), pltpu.VMEM((tt, width), F32)],
        compiler_params=_params(2),
        name="lru_mixer",
    )(h, g.reshape(1, d), w_in, conv_w, row(conv_b), wa, row(ba), wx, row(bx), row(lam), w_out)


def _rope(x, cos, sin):
    half = x.shape[1] // 2
    x1, x2 = x[:, :half], x[:, half:]
    return jnp.concatenate([x1 * cos - x2 * sin, x2 * cos + x1 * sin], axis=1)


def _ret_kernel(z_ref, v_ref, h_ref, cos_ref, sin_ref, wout_ref, o_ref, state, y_scr, dmask_scr, *, c, nchunk):
    log_gammas = [math.log1p(-(2.0 ** (-5.0 - hd))) for hd in range(RET_HEADS)]

    @pl.when(pl.program_id(1) == 0)
    def _():
        state[...] = jnp.zeros_like(state)
        ri = lax.broadcasted_iota(jnp.int32, (c, c), 0)
        ci = lax.broadcasted_iota(jnp.int32, (c, c), 1)
        diff = (ri - ci).astype(F32)
        for hd in range(RET_HEADS):
            dmask_scr[hd] = jnp.where(diff >= 0.0, jnp.exp(log_gammas[hd] * jnp.maximum(diff, 0.0)), 0.0)

    pos = lax.broadcasted_iota(jnp.int32, (c, 1), 0).astype(F32)
    qk_w = RET_HEADS * RET_DK
    v_w = RET_HEADS * RET_DV
    for q_ in range(nchunk):
        rows = slice(q_ * c, (q_ + 1) * c)
        cos = cos_ref[rows, :]
        sin = sin_ref[rows, :]
        for hd in range(RET_HEADS):
            log_gamma = log_gammas[hd]
            q = _rope(z_ref[0, rows, hd * RET_DK:(hd + 1) * RET_DK], cos, sin)
            k = _rope(z_ref[0, rows, qk_w + hd * RET_DK:qk_w + (hd + 1) * RET_DK], cos, sin) * (RET_DK ** -0.5)
            v = v_ref[0, rows, hd * RET_DV:(hd + 1) * RET_DV]
            gt = z_ref[0, rows, 2 * qk_w + hd * RET_DV:2 * qk_w + (hd + 1) * RET_DV]
            inter_dec = jnp.exp(log_gamma * (pos + 1.0))
            k_dec = jnp.exp(log_gamma * (c - 1.0 - pos))
            chunk_dec = math.exp(log_gamma * c)
            qb = q.astype(BF16)
            scores = _dot_nt(qb, k.astype(BF16)) * dmask_scr[hd]
            s_old = state[hd]
            o = _dot(scores.astype(BF16), v) + _dot(qb, s_old.astype(BF16)) * inter_dec
            state[hd] = s_old * chunk_dec + _dot_tn((k * k_dec).astype(BF16), v)
            y = _silu(gt) * _rms(o)
            y_scr[rows, hd * RET_DV:(hd + 1) * RET_DV] = y.astype(BF16)
    o_ref[0] = h_ref[0] + _dot(y_scr[...], wout_ref[...])


def _ret_layer(h, g, w_in, w_out, cos, sin, c=RET_CHUNK, nchunk=RET_CHUNKS_PER_STEP):
    b_, s_, d = h.shape
    n_in = w_in.shape[1]
    v_w = w_out.shape[0]
    tt = c * nchunk
    qk_w = RET_HEADS * RET_DK
    z, v = _norm_matmul(h.reshape(b_ * s_, d), g, w_in.astype(BF16), IN_PROJ_ROWS,
                        bf16_cols=(2 * qk_w, 2 * qk_w + v_w))
    z = z.reshape(b_, s_, n_in - v_w)
    v = v.reshape(b_, s_, v_w)
    return pl.pallas_call(
        functools.partial(_ret_kernel, c=c, nchunk=nchunk),
        grid=(b_, s_ // tt),
        in_specs=[pl.BlockSpec((1, tt, n_in - v_w), lambda bi, ti: (bi, ti, 0)),
                  pl.BlockSpec((1, tt, v_w), lambda bi, ti: (bi, ti, 0)),
                  pl.BlockSpec((1, tt, d), lambda bi, ti: (bi, ti, 0)),
                  pl.BlockSpec((tt, RET_DK // 2), lambda bi, ti: (ti, 0)),
                  pl.BlockSpec((tt, RET_DK // 2), lambda bi, ti: (ti, 0)),
                  pl.BlockSpec((v_w, d), lambda bi, ti: (0, 0), pipeline_mode=pl.Buffered(1))],
        out_specs=pl.BlockSpec((1, tt, d), lambda bi, ti: (bi, ti, 0)),
        out_shape=jax.ShapeDtypeStruct((b_, s_, d), F32),
        scratch_shapes=[pltpu.VMEM((RET_HEADS, RET_DK, RET_DV), F32),
                        pltpu.VMEM((tt, v_w), BF16),
                        pltpu.VMEM((RET_HEADS, c, c), F32)],
        compiler_params=_params(2),
        name="retention_mixer",
    )(z, v, h, cos, sin, w_out.astype(BF16))


GDN_CHUNKS_PER_STEP = 2
GDN_SIDE_UNITS = 3


def _interleave(main, side, per_step):
    for _ in main:
        for _ in range(per_step):
            next(side, None)
    for _ in side:
        pass


def _gdn_kernel(z_ref, h_ref, cw_ref, alog_ref, dtb_ref, nw_ref, wout_ref, o_ref,
                tail, qkv, qn, kn, state, y_scr, *, c, nchunk):
    nq, nv, dh = GDN_QK_HEADS, GDN_V_HEADS, GDN_D
    qk_w = nq * dh
    conv_ch = 2 * qk_w + nv * dh
    rep = nv // nq
    n = c // SUBLANES

    @pl.when(pl.program_id(1) == 0)
    def _():
        tail[...] = jnp.zeros_like(tail)
        state[...] = jnp.zeros_like(state)

    ri = lax.broadcasted_iota(jnp.int32, (c, c), 0)
    ci = lax.broadcasted_iota(jnp.int32, (c, c), 1)
    eye = ri == ci
    strict = ri > ci
    same = lambda sh: (ri >> sh) == (ci >> sh)
    levels = []
    sh = 4
    while (1 << sh) < c:
        levels.append(same(sh + 1) & jnp.logical_not(same(sh)) & strict)
        sh += 1
    diag_blocks = same(4) & strict
    plus_eye = lambda p: jnp.where(eye, 1.0, p).astype(BF16)
    heads = range(nv)

    def prep(q):
        ctx = dict(prods=[], x0b=[], f0b=[], attnb=[], rhsb=[], q_dec=[], k_decb=[], g_tot=[],
                   lowb=[[] for _ in levels])
        for k in range(conv_ch // LANES):
            xs = [z_ref[q, k, _slab_group(j, n), :] for j in range(n)]
            for j, xc in enumerate(_conv_steps(xs, cw_ref, tail, slice(k * LANES, (k + 1) * LANES))):
                qkv[q, k, _slab_group(j, n), :] = _silu(xc)
            yield
        zs = _slab_get(z_ref.at[q, (conv_ch + nv * dh) // LANES], n)
        beta_all = _sigmoid(zs)
        gcum = _cumsum_rows(-jnp.exp(alog_ref[...]) * _softplus(zs + dtb_ref[...]))
        gcum_t = gcum.T
        yield
        for j in range(nq):
            qj = _slab_get(qkv.at[q, j], n)
            kj = _slab_get(qkv.at[q, nq + j], n)
            qn[q, j] = qj * lax.rsqrt(jnp.sum(qj * qj, axis=-1, keepdims=True) + L2_EPS) * (dh ** -0.5)
            kn[q, j] = kj * lax.rsqrt(jnp.sum(kj * kj, axis=-1, keepdims=True) + L2_EPS)
            ctx["prods"].append(_dot_nt(jnp.concatenate([kn[q, j], qn[q, j]], axis=0).astype(BF16),
                                        kn[q, j].astype(BF16)))
            yield
        for i in heads:
            j = i // rep
            beta = beta_all[:, i:i + 1]
            g_col = gcum[:, nv + i:nv + i + 1]
            g_row = gcum_t[nv + i:nv + i + 1, :]
            g_end = g_col[c - 1:c, :]
            decay = jnp.where(ri >= ci, jnp.exp(jnp.minimum(g_col - g_row, 0.0)), 0.0)
            a_mat = beta * ctx["prods"][j][:c] * decay
            x0 = jnp.where(diag_blocks, -a_mat, 0.0)
            ctx["x0b"].append(x0.astype(BF16))
            ctx["f0b"].append(jnp.where(eye, 1.0, x0).astype(BF16))
            for lvl, mask in enumerate(levels):
                ctx["lowb"][lvl].append(jnp.where(mask, a_mat, 0.0).astype(BF16))
            ctx["attnb"].append((ctx["prods"][j][c:] * decay).astype(BF16))
            exp_g = jnp.exp(g_col)
            v = _slab_get(qkv.at[q, 2 * nq + i], n)
            ctx["rhsb"].append(jnp.concatenate([v * beta, kn[q, j] * (beta * exp_g)], axis=1).astype(BF16))
            ctx["q_dec"].append(qn[q, j] * exp_g)
            ctx["k_decb"].append((kn[q, j] * jnp.exp(g_end - g_col)).astype(BF16))
            ctx["g_tot"].append(jnp.exp(g_end))
            yield
        ctxs[q] = ctx

    def mxu(q):
        cx = ctxs[q]
        p1 = [_dot(x, x) for x in cx["x0b"]]
        yield
        m1 = [_dot(f0, plus_eye(p)).astype(BF16) for f0, p in zip(cx["f0b"], p1)]
        p1b = [p.astype(BF16) for p in p1]
        yield
        p2 = [_dot(p, p) for p in p1b]
        yield
        p2b = [p.astype(BF16) for p in p2]
        p3 = [_dot(pb, pb) for pb in p2b]
        yield
        m2 = [_dot(plus_eye(a), plus_eye(b_)).astype(BF16) for a, b_ in zip(p2, p3)]
        yield
        tb = [_dot(u, w).astype(BF16) for u, w in zip(m1, m2)]
        yield
        for lows in cx["lowb"]:
            corr = [jnp.where(eye, 1.0, -_dot(low, t)).astype(BF16) for low, t in zip(lows, tb)]
            yield
            tb = [_dot(t, g).astype(BF16) for t, g in zip(tb, corr)]
            yield
        uw = [_dot(t, r) for t, r in zip(tb, cx["rhsb"])]
        yield
        s_old = [state[i] for i in heads]
        wq = [_dot(jnp.concatenate([uw[i][:, dh:], cx["q_dec"][i]], axis=0).astype(BF16),
                   s_old[i].astype(BF16)) for i in heads]
        yield
        v_newb = [(uw[i][:, :dh] - wq[i][:c]).astype(BF16) for i in heads]
        cx["o"] = [wq[i][c:] + _dot(cx["attnb"][i], v_newb[i]) for i in heads]
        yield
        for i in heads:
            state[i] = s_old[i] * cx["g_tot"][i] + _dot_tn(cx["k_decb"][i], v_newb[i])
        yield

    def post(q):
        nw = nw_ref[...]
        for i in heads:
            zg = _slab_get(z_ref.at[q, conv_ch // LANES + i], n)
            y_scr[q * c:(q + 1) * c, i * dh:(i + 1) * dh] = (_rms(ctxs[q]["o"][i]) * nw * _silu(zg)).astype(BF16)
            yield

    ctxs = {}
    for _ in prep(0):
        pass
    for q in range(nchunk):
        side = itertools.chain(post(q - 1) if q > 0 else (), prep(q + 1) if q + 1 < nchunk else ())
        _interleave(mxu(q), side, GDN_SIDE_UNITS)
    for _ in post(nchunk - 1):
        pass
    o_ref[0] = h_ref[0] + _dot(y_scr[...], wout_ref[...])


def _gdn_layer(h, g, w_in, conv_w, a_log, dt_bias, norm_w, w_out, c=GDN_CHUNK, nchunk=GDN_CHUNKS_PER_STEP):
    b_, s_, d = h.shape
    v_w = w_out.shape[0]
    conv_ch = conv_w.shape[1]
    main_w = conv_ch + v_w
    small = w_in.shape[1] - main_w
    w_tail = jnp.pad(w_in[:, main_w:], ((0, 0), (0, LANES - small))).astype(BF16)
    n_cat = main_w + LANES
    z = _norm_matmul(h.reshape(b_ * s_, d), g, w_in.astype(BF16), IN_PROJ_ROWS, slab_rows=c,
                     w_tail=w_tail, n=main_w)
    tt = c * nchunk
    nt = s_ // tt
    pad_row = lambda v: jnp.zeros((1, LANES), F32).at[0, GDN_V_HEADS:2 * GDN_V_HEADS].set(v)
    const2 = lambda bi, ti: (0, 0)
    rows = _slab_rows(c // SUBLANES)
    return pl.pallas_call(
        functools.partial(_gdn_kernel, c=c, nchunk=nchunk),
        grid=(b_, nt),
        in_specs=[pl.BlockSpec((nchunk, n_cat // LANES, rows, LANES), lambda bi, ti: (bi * nt + ti, 0, 0, 0)),
                  pl.BlockSpec((1, tt, d), lambda bi, ti: (bi, ti, 0)),
                  pl.BlockSpec((CONV_WIDTH, conv_ch), const2),
                  pl.BlockSpec((1, LANES), const2),
                  pl.BlockSpec((1, LANES), const2),
                  pl.BlockSpec((1, GDN_D), const2),
                  pl.BlockSpec((v_w, d), const2)],
        out_specs=pl.BlockSpec((1, tt, d), lambda bi, ti: (bi, ti, 0)),
        out_shape=jax.ShapeDtypeStruct((b_, s_, d), F32),
        scratch_shapes=[pltpu.VMEM((CONV_WIDTH - 1, SUBLANES, conv_ch), F32),
                        pltpu.VMEM((nchunk, conv_ch // LANES, rows, LANES), F32),
                        pltpu.VMEM((nchunk, GDN_QK_HEADS, c, GDN_D), F32),
                        pltpu.VMEM((nchunk, GDN_QK_HEADS, c, GDN_D), F32),
                        pltpu.VMEM((GDN_V_HEADS, GDN_D, GDN_D), F32),
                        pltpu.VMEM((tt, v_w), BF16)],
        compiler_params=_params(2),
        name="gdn_mixer",
    )(z, h, conv_w, pad_row(a_log), pad_row(dt_bias), norm_w.reshape(1, GDN_D),
      w_out.astype(BF16))


def kernel(x, mix_norm, mlp_norm, mlp_w_up, mlp_w_down, lru_w_in, lru_conv_w, lru_conv_b, lru_wa, lru_ba, lru_wx, lru_bx, lru_lambda, lru_w_out, ret_w_in, ret_w_out, gdn_w_in, gdn_conv_w, gdn_a_log, gdn_dt_bias, gdn_norm, gdn_w_out, final_norm):
    b_, s_, d = x.shape
    depth = mix_norm.shape[0]
    pos = jnp.arange(s_, dtype=F32)
    inv_freq = ROPE_BASE ** (-jnp.arange(0, RET_DK, 2, dtype=F32) / RET_DK)
    ang = pos[:, None] * inv_freq[None, :]
    cos, sin = jnp.cos(ang), jnp.sin(ang)
    lru_w = [w.astype(BF16) for w in (lru_w_in, lru_wa, lru_wx, lru_w_out)]
    h = x
    for i in range(depth):
        kind, j = i % 3, i // 3
        if kind == 0:
            h = _lru_layer(h, mix_norm[i], j, *lru_w, lru_conv_w[j], lru_conv_b[j],
                           lru_ba[j].reshape(-1), lru_bx[j].reshape(-1), lru_lambda[j])
        elif kind == 1:
            h = _ret_layer(h, mix_norm[i], ret_w_in[j], ret_w_out[j], cos, sin)
        else:
            h = _gdn_layer(h, mix_norm[i], gdn_w_in[j], gdn_conv_w[j], gdn_a_log[j],
                           gdn_dt_bias[j], gdn_norm[j], gdn_w_out[j])
        h = _mlp(h.reshape(b_ * s_, d), mlp_norm[i], mlp_w_up, mlp_w_down, i, final_norm, i == depth - 1,
                 MLP_ROWS, MLP_FF_BLOCK).reshape(b_, s_, d)
    return h
```
